```python
import math
import jax
import jax.numpy as jnp
from jax import lax

D_MODEL = 1024
BATCH = 2
SEQ = 16384
DEPTH = 4

GRID_W = 64
CTX_LEN = 256
N_MOD = 9
D_FF = 2816
NORM_EPS = 1e-6

DN_HEADS = 4
DN_HEAD_DIM = 128
DN_WIDTH = DN_HEADS * DN_HEAD_DIM
DN_CHUNK = 64
SHORT_CONV = 5
DN_IN = 4 * DN_WIDTH + 4 * DN_HEADS

RW_HEADS = 8
RW_HEAD_DIM = 64
RW_WIDTH = RW_HEADS * RW_HEAD_DIM
DECAY_LORA = 64
ICLR_LORA = 64
GATE_LORA = 128
RW_GN_EPS = 64e-5
RW_IN = 3 * RW_WIDTH + 2 * DECAY_LORA + 2 * ICLR_LORA + GATE_LORA

MIX_WIDTH = DN_WIDTH + RW_WIDTH
N_IN = DN_IN + RW_IN

kernel_name = 'hybrid_deltanet_rwkv7_macaron_dit'


def rms_norm(x, gain):
    xf = x.astype(jnp.float32)
    y = xf * lax.rsqrt(jnp.mean(xf * xf, axis=-1, keepdims=True) + NORM_EPS)
    return (y * gain.astype(jnp.float32)).astype(x.dtype)


def l2_normalize(t):
    return t * lax.rsqrt(jnp.sum(t * t, axis=-1, keepdims=True) + NORM_EPS)


def modulate(h, shift, scale):
    return h * (1.0 + scale) + shift


def swiglu(h, w13, w2):
    gate, up = jnp.split(h @ w13, 2, axis=-1)
    return (jax.nn.silu(gate) * up) @ w2


def maybe_flip(t, axis, backward):
    return jnp.flip(t, axis) if backward else t


def raster_to_colmajor(t, rows):
    b, l, ch = t.shape
    return t.reshape(b, rows, GRID_W, ch).swapaxes(1, 2).reshape(b, l, ch)


def colmajor_to_raster(t, rows):
    b, l, ch = t.shape
    return t.reshape(b, GRID_W, rows, ch).swapaxes(1, 2).reshape(b, l, ch)


def centred_depthwise_conv(t, w):
    return lax.conv_general_dilated(t, w[:, None, :].astype(t.dtype), window_strides=(1,), padding='SAME',
                                    dimension_numbers=('NWC', 'WIO', 'NWC'), feature_group_count=t.shape[-1])


def centred_token_shift(t, mu):
    prev = jnp.pad(t, ((0, 0), (1, 0), (0, 0)))[:, :-1]
    nxt = jnp.pad(t, ((0, 0), (0, 1), (0, 0)))[:, 1:]
    return t + mu[0] * (prev - t) + mu[1] * (nxt - t)


def gated_delta_rule_chunked(q, k, v, g, beta, state):
    bsz, nh, seqlen, dk = q.shape
    dv = v.shape[-1]
    n_chunks = seqlen // DN_CHUNK
    f32 = jnp.float32
    q, k, v = [t.astype(f32).reshape(bsz, nh, n_chunks, DN_CHUNK, t.shape[-1]) for t in (q, k, v)]
    g = g.astype(f32).reshape(bsz, nh, n_chunks, DN_CHUNK)
    beta = beta.astype(f32).reshape(bsz, nh, n_chunks, DN_CHUNK)
    gc = jnp.cumsum(g, axis=-1)
    idx = jnp.arange(DN_CHUNK)
    lower = idx[:, None] >= idx[None, :]
    strict = idx[:, None] > idx[None, :]
    decay = jnp.exp(jnp.where(lower, gc[..., :, None] - gc[..., None, :], -jnp.inf))
    a_strict = jnp.where(strict, jnp.einsum('bhnid,bhnjd->bhnij', k, k) * decay * beta[..., :, None], 0.0)
    rhs = jnp.concatenate([k * (beta * jnp.exp(gc))[..., None], v * beta[..., None]], axis=-1)
    sol = lax.linalg.triangular_solve(a_strict, rhs, left_side=True, lower=True, unit_diagonal=True)
    w, u = sol[..., :dk], sol[..., dk:]
    attn = jnp.einsum('bhnid,bhnjd->bhnij', q, k) * decay
    q_dec = q * jnp.exp(gc)[..., None]
    k_dec = k * jnp.exp(gc[..., -1:] - gc)[..., None]
    chunk_decay = jnp.exp(gc[..., -1])

    def step(s, inp):
        w_n, u_n, attn_n, qd_n, kd_n, cd_n = inp
        v_new = u_n - jnp.einsum('bhcd,bhde->bhce', w_n, s)
        o_n = jnp.einsum('bhcd,bhde->bhce', qd_n, s) + jnp.einsum('bhij,bhje->bhie', attn_n, v_new)
        s = s * cd_n[..., None, None] + jnp.einsum('bhcd,bhce->bhde', kd_n, v_new)
        return s, o_n

    xs = tuple(jnp.moveaxis(t, 2, 0) for t in (w, u, attn, q_dec, k_dec, chunk_decay))
    state, o = lax.scan(step, state.astype(f32), xs)
    return jnp.moveaxis(o, 0, 2).reshape(bsz, nh, seqlen, dv), state


def deltanet_features(p, conv_w, a_log, dt_bias):
    bsz, seqlen, _ = p.shape
    p = p.astype(jnp.float32)
    qkv = jax.nn.silu(centred_depthwise_conv(p[..., :3 * DN_WIDTH], conv_w))
    q, k, v = [t.reshape(bsz, seqlen, DN_HEADS, DN_HEAD_DIM) for t in jnp.split(qkv, 3, axis=-1)]
    q = l2_normalize(q) * DN_HEAD_DIM ** -0.5
    k = l2_normalize(k)
    z = p[..., 3 * DN_WIDTH:4 * DN_WIDTH].reshape(bsz, seqlen, DN_HEADS, DN_HEAD_DIM)
    ab = p[..., 4 * DN_WIDTH:].reshape(bsz, seqlen, 2, 2, DN_HEADS)
    g = -jnp.exp(a_log) * jax.nn.softplus(ab[:, :, 0] + dt_bias)
    beta = jax.nn.sigmoid(ab[:, :, 1])
    bhl = lambda t: jnp.transpose(t, (0, 2, 1, 3))
    return bhl(q), bhl(k), bhl(v), jnp.transpose(g, (0, 2, 3, 1)), jnp.transpose(beta, (0, 2, 3, 1)), z


def gated_head_norm(o, z, w):
    o = jnp.transpose(o, (0, 2, 1, 3))
    o = o * lax.rsqrt(jnp.mean(o * o, axis=-1, keepdims=True) + NORM_EPS) * w
    o = o * jax.nn.silu(z)
    return o.reshape(o.shape[0], o.shape[1], DN_WIDTH)


def deltanet_group(p_ctx, p_lat, conv_w, a_log, dt_bias, norm_w):
    qc, kc, vc, gc, bc, zc = deltanet_features(p_ctx, conv_w, a_log, dt_bias)
    ql, kl, vl, gl, bl, zl = deltanet_features(p_lat, conv_w, a_log, dt_bias)
    zero = jnp.zeros((qc.shape[0], DN_HEADS, DN_HEAD_DIM, DN_HEAD_DIM), jnp.float32)
    o_ctx = jnp.zeros_like(vc)
    o_lat = jnp.zeros_like(vl)
    for d in range(2):
        bwd = d == 1
        oc, s_ctx = gated_delta_rule_chunked(*[maybe_flip(t, 2, bwd) for t in (qc, kc, vc, gc[:, d], bc[:, d])], zero)
        ol, _ = gated_delta_rule_chunked(*[maybe_flip(t, 2, bwd) for t in (ql, kl, vl, gl[:, d], bl[:, d])], s_ctx)
        o_ctx = o_ctx + maybe_flip(oc, 2, bwd)
        o_lat = o_lat + maybe_flip(ol, 2, bwd)
    return gated_head_norm(o_ctx, zc, norm_w), gated_head_norm(o_lat, zl, norm_w)


def rwkv7_scan(r, w, k, v, kk, a, state):
    def step(s, inp):
        r_t, w_t, k_t, v_t, kk_t, a_t = inp
        sa = jnp.einsum('bhvk,bhk->bhv', s, kk_t)
        s = s * w_t[:, :, None, :] - sa[..., None] * (kk_t * a_t)[:, :, None, :] + v_t[..., None] * k_t[:, :, None, :]
        return s, jnp.einsum('bhvk,bhk->bhv', s, r_t)

    xs = tuple(jnp.moveaxis(t, 1, 0) for t in (r, w, k, v, kk, a))
    state, y = lax.scan(step, state, xs)
    return jnp.moveaxis(y, 0, 1), state


def rwkv7_features(p, mu, w0, w2, a0, a2, g2, k_k, k_a):
    bsz, seqlen, _ = p.shape
    heads = lambda t: t.reshape(*t.shape[:-1], RW_HEADS, RW_HEAD_DIM)
    p = centred_token_shift(p.astype(jnp.float32), mu)
    r, k, v = jnp.split(p[..., :3 * RW_WIDTH], 3, axis=-1)
    o = 3 * RW_WIDTH
    lw = p[..., o:o + 2 * DECAY_LORA].reshape(bsz, seqlen, 2, DECAY_LORA)
    o += 2 * DECAY_LORA
    la = p[..., o:o + 2 * ICLR_LORA].reshape(bsz, seqlen, 2, ICLR_LORA)
    o += 2 * ICLR_LORA
    lg = p[..., o:]
    w_log = -jax.nn.softplus(-(w0 + jnp.einsum('bldr,drc->bldc', jnp.tanh(lw), w2))) - 0.5
    decay = jnp.exp(-jnp.exp(w_log))
    a = jax.nn.sigmoid(a0 + jnp.einsum('bldr,drc->bldc', la, a2))
    gate = jax.nn.sigmoid(lg) @ g2
    kk = l2_normalize(heads(k * k_k))
    k_dir = k[:, :, None, :] * (1.0 + (a - 1.0) * k_a)
    return heads(r), heads(v), kk, heads(decay), heads(a), heads(k_dir), gate


def head_group_norm(y, weight, bias):
    mean = jnp.mean(y, axis=-1, keepdims=True)
    yc = y - mean
    yn = yc * lax.rsqrt(jnp.mean(yc * yc, axis=-1, keepdims=True) + RW_GN_EPS)
    return yn.reshape(*y.shape[:-2], RW_WIDTH) * weight + bias


def rwkv7_group(p_ctx, p_lat, rows, mu, w0, w2, a0, a2, g2, k_k, k_a, u, ln):
    rc, vc, kkc, wc, ac, kc, gc = rwkv7_features(p_ctx, mu, w0, w2, a0, a2, g2, k_k, k_a)
    rl, vl, kkl, wl, al, kl, gl = rwkv7_features(raster_to_colmajor(p_lat, rows), mu, w0, w2, a0, a2, g2, k_k, k_a)
    zero = jnp.zeros((rc.shape[0], RW_HEADS, RW_HEAD_DIM, RW_HEAD_DIM), jnp.float32)
    y_ctx = jnp.zeros_like(rc)
    y_lat = jnp.zeros_like(rl)
    for d in range(2):
        bwd = d == 1
        yc, s_ctx = rwkv7_scan(*[maybe_flip(t, 1, bwd) for t in (rc, wc[:, :, d], kc[:, :, d], vc, kkc, ac[:, :, d])], zero)
        yl, _ = rwkv7_scan(*[maybe_flip(t, 1, bwd) for t in (rl, wl[:, :, d], kl[:, :, d], vl, kkl, al[:, :, d])], s_ctx)
        y_ctx = y_ctx + maybe_flip(yc, 1, bwd) + jnp.sum(rc * kc[:, :, d] * u[d], axis=-1, keepdims=True) * vc
        y_lat = y_lat + maybe_flip(yl, 1, bwd) + jnp.sum(rl * kl[:, :, d] * u[d], axis=-1, keepdims=True) * vl
    out_ctx = head_group_norm(y_ctx, ln[0], ln[1]) * gc
    out_lat = colmajor_to_raster(head_group_norm(y_lat, ln[0], ln[1]) * gl, rows)
    return out_ctx, out_lat


def token_mixing(h_ctx, h_lat, rows, w_in, dn_conv, dn_a_log, dn_dt_bias, dn_norm, rw_mu, rw_w0, rw_w2,
                 rw_a0, rw_a2, rw_g2, rw_kk, rw_ka, rw_u, rw_ln, w_out):
    p_ctx = h_ctx @ w_in
    p_lat = h_lat @ w_in
    dn_ctx, dn_lat = deltanet_group(p_ctx[..., :DN_IN], p_lat[..., :DN_IN], dn_conv, dn_a_log, dn_dt_bias, dn_norm)
    rw_ctx, rw_lat = rwkv7_group(p_ctx[..., DN_IN:], p_lat[..., DN_IN:], rows, rw_mu, rw_w0, rw_w2,
                                 rw_a0, rw_a2, rw_g2, rw_kk, rw_ka, rw_u, rw_ln)
    out_ctx = jnp.concatenate([dn_ctx, rw_ctx], axis=-1).astype(h_ctx.dtype) @ w_out
    out_lat = jnp.concatenate([dn_lat, rw_lat], axis=-1).astype(h_lat.dtype) @ w_out
    return out_ctx, out_lat


def setup_inputs(seed: int = 0) -> dict:
    key = jax.random.key(seed)
    ks = iter(jax.random.split(key, 32))
    nrm = lambda shape, scale: scale * jax.random.normal(next(ks), shape, jnp.float32)
    unif = lambda shape, lo, hi: jax.random.uniform(next(ks), shape, jnp.float32, lo, hi)
    dt = jnp.exp(unif((DEPTH, 2, DN_HEADS), math.log(1e-3), math.log(1e-1)))
    return {
        'x': nrm((BATCH, SEQ, D_MODEL), 1.0),
        'c': nrm((BATCH, D_MODEL), 1.0),
        'ctx': nrm((BATCH, CTX_LEN, D_MODEL), 1.0),
        'c_ctx': nrm((D_MODEL,), 1.0),
        'ada_w': nrm((DEPTH, D_MODEL, N_MOD * D_MODEL), D_MODEL ** -0.5),
        'ada_b': nrm((DEPTH, N_MOD * D_MODEL), 0.02),
        'norm_g': 1.0 + nrm((DEPTH, 3, D_MODEL), 0.05),
        'ffn_w13': nrm((DEPTH, 2, D_MODEL, 2 * D_FF), D_MODEL ** -0.5),
        'ffn_w2': nrm((DEPTH, 2, D_FF, D_MODEL), D_FF ** -0.5),
        'w_in': nrm((DEPTH, D_MODEL, N_IN), D_MODEL ** -0.5),
        'dn_conv': nrm((DEPTH, SHORT_CONV, 3 * DN_WIDTH), SHORT_CONV ** -0.5),
        'dn_a_log': jnp.log(unif((DEPTH, 2, DN_HEADS), 1.0, 16.0)),
        'dn_dt_bias': dt + jnp.log(-jnp.expm1(-dt)),
        'dn_norm': 1.0 + nrm((DEPTH, DN_HEAD_DIM), 0.05),
        'rw_mu': unif((DEPTH, 2, RW_IN), 0.0, 0.5),
        'rw_w0': unif((DEPTH, 2, RW_WIDTH), -5.5, -0.5),
        'rw_w2': nrm((DEPTH, 2, DECAY_LORA, RW_WIDTH), 0.1),
        'rw_a0': nrm((DEPTH, 2, RW_WIDTH), 0.1),
        'rw_a2': nrm((DEPTH, 2, ICLR_LORA, RW_WIDTH), 0.1),
        'rw_g2': nrm((DEPTH, GATE_LORA, RW_WIDTH), GATE_LORA ** -0.5),
        'rw_kk': 0.85 + nrm((DEPTH, RW_WIDTH), 0.05),
        'rw_ka': 1.0 + nrm((DEPTH, RW_WIDTH), 0.05),
        'rw_u': nrm((DEPTH, 2, RW_HEADS, RW_HEAD_DIM), 0.1),
        'rw_ln': jnp.stack([1.0 + nrm((DEPTH, RW_WIDTH), 0.05), nrm((DEPTH, RW_WIDTH), 0.01)], axis=1),
        'w_out': nrm((DEPTH, MIX_WIDTH, D_MODEL), MIX_WIDTH ** -0.5),
        'final_norm': 1.0 + nrm((D_MODEL,), 0.05),
    }


def reference(x, c, ctx, c_ctx, ada_w, ada_b, norm_g, ffn_w13, ffn_w2, w_in, dn_conv, dn_a_log, dn_dt_bias,
              dn_norm, rw_mu, rw_w0, rw_w2, rw_a0, rw_a2, rw_g2, rw_kk, rw_ka, rw_u, rw_ln, w_out, final_norm):
    rows = x.shape[1] // GRID_W
    h_lat, h_ctx = x, ctx
    s_lat = jax.nn.silu(c)[:, None, :]
    s_ctx = jax.nn.silu(c_ctx)[None, None, :]
    for i in range(DEPTH):
        last = i == DEPTH - 1
        m_lat = jnp.split(s_lat @ ada_w[i] + ada_b[i], N_MOD, axis=-1)
        m_ctx = jnp.split(s_ctx @ ada_w[i] + ada_b[i], N_MOD, axis=-1)
        h_lat = h_lat + 0.5 * m_lat[2] * swiglu(modulate(rms_norm(h_lat, norm_g[i, 0]), m_lat[0], m_lat[1]),
                                                ffn_w13[i, 0], ffn_w2[i, 0])
        h_ctx = h_ctx + 0.5 * m_ctx[2] * swiglu(modulate(rms_norm(h_ctx, norm_g[i, 0]), m_ctx[0], m_ctx[1]),
                                                ffn_w13[i, 0], ffn_w2[i, 0])
        mix_ctx, mix_lat = token_mixing(
            modulate(rms_norm(h_ctx, norm_g[i, 1]), m_ctx[3], m_ctx[4]),
            modulate(rms_norm(h_lat, norm_g[i, 1]), m_lat[3], m_lat[4]),
            rows, w_in[i], dn_conv[i], dn_a_log[i], dn_dt_bias[i], dn_norm[i], rw_mu[i], rw_w0[i], rw_w2[i],
            rw_a0[i], rw_a2[i], rw_g2[i], rw_kk[i], rw_ka[i], rw_u[i], rw_ln[i], w_out[i])
        h_lat = h_lat + m_lat[5] * mix_lat
        h_lat = h_lat + 0.5 * m_lat[8] * swiglu(modulate(rms_norm(h_lat, norm_g[i, 2]), m_lat[6], m_lat[7]),
                                                ffn_w13[i, 1], ffn_w2[i, 1])
        if not last:
            h_ctx = h_ctx + m_ctx[5] * mix_ctx
            h_ctx = h_ctx + 0.5 * m_ctx[8] * swiglu(modulate(rms_norm(h_ctx, norm_g[i, 2]), m_ctx[6], m_ctx[7]),
                                                    ffn_w13[i, 1], ffn_w2[i, 1])
    return rms_norm(h_lat, final_norm)
```

```python
import functools

import jax
import jax.numpy as jnp
from jax import lax
from jax.experimental import pallas as pl
from jax.experimental.pallas import tpu as pltpu

F32 = jnp.float32
BF16 = jnp.bfloat16
HIGHEST = lax.Precision.HIGHEST

NORM_EPS = 1e-6
RW_GN_EPS = 64e-5
N_MOD = 9
GRID_W = 64
SHORT_CONV = 5

DN_HEADS = 4
DN_HEAD_DIM = 128
DN_WIDTH = DN_HEADS * DN_HEAD_DIM
DN_AB_PAD = 128
DN_P_WIDTH = 4 * DN_WIDTH + DN_AB_PAD
DN_CHUNK = 128

RW_HEADS = 8
RW_HEAD_DIM = 64
RW_WIDTH = RW_HEADS * RW_HEAD_DIM
RW_LORA = 128
RW_P_WIDTH = 3 * RW_WIDTH + 3 * RW_LORA
RW_CHUNK = 64
RW_PAIR = 2 * RW_HEAD_DIM
RW_PAIRS = RW_WIDTH // RW_PAIR

HALO = 8
V7X_VMEM_LIMIT = 56 * 1024 * 1024


def _params(*sem):
    return pltpu.CompilerParams(dimension_semantics=sem, vmem_limit_bytes=V7X_VMEM_LIMIT)


def _dot(a, b, precision=None):
    return jnp.dot(a, b, preferred_element_type=F32, precision=precision)


def _dot_nt(a, b):
    return lax.dot_general(a, b, (((1,), (1,)), ((), ())), preferred_element_type=F32)


def _dot_tn(a, b):
    return lax.dot_general(a, b, (((0,), (0,)), ((), ())), preferred_element_type=F32)


def _bdot(a, b):
    return _dot(a.astype(BF16), b.astype(BF16))


def _silu(x):
    return x * jax.nn.sigmoid(x)


def _softplus(x):
    return jnp.maximum(x, 0.0) + jnp.log1p(jnp.exp(-jnp.abs(x)))


def _rms_modulate(h, gain, shift, scale):
    y = h * lax.rsqrt(jnp.mean(h * h, axis=-1, keepdims=True) + NORM_EPS) * gain
    return y * (1.0 + scale) + shift


def _unit_tri_inverse(a, size):
    n = a.shape[0]
    diff = lax.broadcasted_iota(jnp.int32, (n, n), 0) ^ lax.broadcasted_iota(jnp.int32, (n, n), 1)
    t = jnp.where(diff == 0, 1.0, jnp.where(diff == 1, -a, 0.0))
    level = 1
    while (1 << level) < size:
        l = jnp.where((diff >> level) == 1, a, 0.0)
        t = t - _bdot(_bdot(t, l), t)
        level += 1
    return t


def _shift_rows(x, halo_prev, halo_next, s):
    rows = x.shape[0]
    row8 = lax.broadcasted_iota(jnp.int32, (HALO, x.shape[1]), 0)
    back = pltpu.roll(x, s, 0)
    hp = pltpu.roll(halo_prev, s, 0)
    back = jnp.concatenate([jnp.where(row8 < s, hp, back[:HALO]), back[HALO:]], axis=0)
    fwd = pltpu.roll(x, rows - s, 0)
    hn = pltpu.roll(halo_next, HALO - s, 0)
    fwd = jnp.concatenate([fwd[:rows - HALO], jnp.where(row8 >= HALO - s, hn, fwd[rows - HALO:])], axis=0)
    return back, fwd


def _adaln_kernel(s_ref, w_ref, b_ref, o_ref):
    s = _silu(s_ref[...])
    o_ref[0] = _dot(s, w_ref[0], HIGHEST) + b_ref[0]


def _adaln(s_rows, ada_w, ada_b):
    depth, d, n = ada_w.shape
    tn = n // 8
    return pl.pallas_call(
        _adaln_kernel,
        grid=(depth, n // tn),
        in_specs=[
            pl.BlockSpec((8, d), lambda l, j: (0, 0)),
            pl.BlockSpec((1, d, tn), lambda l, j: (l, 0, j)),
            pl.BlockSpec((1, 1, tn), lambda l, j: (l, 0, j)),
        ],
        out_specs=pl.BlockSpec((1, 8, tn), lambda l, j: (l, 0, j)),
        out_shape=jax.ShapeDtypeStruct((depth, 8, n), F32),
        compiler_params=_params("parallel", "parallel"),
        name="adaln",
    )(s_rows, ada_w, ada_b.reshape(depth, 1, n))


def _ffn_kernel(h_ref, mod_ref, g_ref, w1_ref, w3_ref, w2_ref, o_ref, xn_ref, acc_ref, *, row0):
    k = pl.program_id(2)

    @pl.when(k == 0)
    def _():
        xn = _rms_modulate(h_ref[0], g_ref[...], mod_ref[0, row0:row0 + 1, :], mod_ref[0, row0 + 1:row0 + 2, :])
        xn_ref[...] = xn.astype(BF16)
        acc_ref[...] = jnp.zeros_like(acc_ref)

    xn = xn_ref[...]
    gate = _dot(xn, w1_ref[...])
    up = _dot(xn, w3_ref[...])
    acc_ref[...] += _dot((_silu(gate) * up).astype(BF16), w2_ref[...])

    @pl.when(k == pl.num_programs(2) - 1)
    def _():
        o_ref[0] = h_ref[0] + 0.5 * mod_ref[0, row0 + 2:row0 + 3, :] * acc_ref[...]


def _ffn(h, mod, gain, w13, w2, *, row0, tm):
    bsz, seqlen, d = h.shape
    dff = w2.shape[0]
    tf = dff // 2
    nk = dff // tf
    return pl.pallas_call(
        functools.partial(_ffn_kernel, row0=row0),
        grid=(bsz, seqlen // tm, nk),
        in_specs=[
            pl.BlockSpec((1, tm, d), lambda b, i, k: (b, i, 0)),
            pl.BlockSpec((1, N_MOD, d), lambda b, i, k: (b, 0, 0)),
            pl.BlockSpec((1, d), lambda b, i, k: (0, 0)),
            pl.BlockSpec((d, tf), lambda b, i, k: (0, k)),
            pl.BlockSpec((d, tf), lambda b, i, k: (0, k + nk)),
            pl.BlockSpec((tf, d), lambda b, i, k: (k, 0)),
        ],
        out_specs=pl.BlockSpec((1, tm, d), lambda b, i, k: (b, i, 0)),
        out_shape=jax.ShapeDtypeStruct(h.shape, F32),
        scratch_shapes=[pltpu.VMEM((tm, d), BF16), pltpu.VMEM((tm, d), F32)],
        compiler_params=_params("parallel", "parallel", "arbitrary"),
        name="ffn",
    )(h, mod, gain.reshape(1, d), w13, w13, w2)


def _proj_kernel(h_ref, mod_ref, g_ref, wdn_ref, wrw_ref, pdn_ref, prw_ref):
    xn = _rms_modulate(h_ref[0], g_ref[...], mod_ref[0, 3:4, :], mod_ref[0, 4:5, :]).astype(BF16)
    pdn_ref[0] = _dot(xn, wdn_ref[...])
    prw_ref[0] = _dot(xn, wrw_ref[...])


def _proj(h, mod, gain, w_dn, w_rw, *, tm):
    bsz, seqlen, d = h.shape
    return pl.pallas_call(
        _proj_kernel,
        grid=(bsz, seqlen // tm),
        in_specs=[
            pl.BlockSpec((1, tm, d), lambda b, i: (b, i, 0)),
            pl.BlockSpec((1, N_MOD, d), lambda b, i: (b, 0, 0)),
            pl.BlockSpec((1, d), lambda b, i: (0, 0)),
            pl.BlockSpec((d, DN_P_WIDTH), lambda b, i: (0, 0)),
            pl.BlockSpec((d, RW_P_WIDTH), lambda b, i: (0, 0)),
        ],
        out_specs=[
            pl.BlockSpec((1, tm, DN_P_WIDTH), lambda b, i: (b, i, 0)),
            pl.BlockSpec((1, tm, RW_P_WIDTH), lambda b, i: (b, i, 0)),
        ],
        out_shape=[
            jax.ShapeDtypeStruct((bsz, seqlen, DN_P_WIDTH), F32),
            jax.ShapeDtypeStruct((bsz, seqlen, RW_P_WIDTH), F32),
        ],
        compiler_params=_params("parallel", "parallel"),
        name="proj",
    )(h, mod, gain.reshape(1, d), w_dn, w_rw)


def _dn_feat_kernel(x_ref, xp_ref, xn_ref, ab_ref, cw_ref, alog_ref, dt_ref, qkv_ref, gb_ref):
    i = pl.program_id(1)
    x = x_ref[0]
    hp = jnp.where(i == 0, 0.0, xp_ref[0])
    hn = jnp.where(i == pl.num_programs(1) - 1, 0.0, xn_ref[0])
    b1, f1 = _shift_rows(x, hp, hn, 1)
    b2, f2 = _shift_rows(x, hp, hn, 2)
    conv = cw_ref[0:1] * b2 + cw_ref[1:2] * b1 + cw_ref[2:3] * x + cw_ref[3:4] * f1 + cw_ref[4:5] * f2
    act = _silu(conv)
    for hd in range(2 * DN_HEADS):
        t = act[:, hd * DN_HEAD_DIM:(hd + 1) * DN_HEAD_DIM]
        t = t * lax.rsqrt(jnp.sum(t * t, axis=-1, keepdims=True) + NORM_EPS)
        if hd < DN_HEADS:
            t = t * DN_HEAD_DIM ** -0.5
        qkv_ref[0, :, hd * DN_HEAD_DIM:(hd + 1) * DN_HEAD_DIM] = t
    qkv_ref[0, :, 2 * DN_WIDTH:] = act[:, 2 * DN_WIDTH:]
    ab = ab_ref[0]
    lane = lax.broadcasted_iota(jnp.int32, ab.shape, 1)
    g = -jnp.exp(alog_ref[...]) * _softplus(ab + dt_ref[...])
    gb_ref[0] = jnp.where(lane < 2 * DN_HEADS, g, jax.nn.sigmoid(ab))


def _dn_features(p_dn, conv_w, alog_row, dt_row, *, rows):
    bsz, seqlen, _ = p_dn.shape
    nblk = seqlen // rows
    hpb = rows // HALO
    return pl.pallas_call(
        _dn_feat_kernel,
        grid=(bsz, nblk),
        in_specs=[
            pl.BlockSpec((1, rows, 3 * DN_WIDTH), lambda b, i: (b, i, 0)),
            pl.BlockSpec((1, HALO, 3 * DN_WIDTH), lambda b, i: (b, jnp.maximum(i * hpb - 1, 0), 0)),
            pl.BlockSpec((1, HALO, 3 * DN_WIDTH), lambda b, i: (b, jnp.minimum((i + 1) * hpb, nblk * hpb - 1), 0)),
            pl.BlockSpec((1, rows, DN_AB_PAD), lambda b, i: (b, i, 4 * DN_WIDTH // DN_AB_PAD)),
            pl.BlockSpec((8, 3 * DN_WIDTH), lambda b, i: (0, 0)),
            pl.BlockSpec((1, DN_AB_PAD), lambda b, i: (0, 0)),
            pl.BlockSpec((1, DN_AB_PAD), lambda b, i: (0, 0)),
        ],
        out_specs=[
            pl.BlockSpec((1, rows, 3 * DN_WIDTH), lambda b, i: (b, i, 0)),
            pl.BlockSpec((1, rows, DN_AB_PAD), lambda b, i: (b, i, 0)),
        ],
        out_shape=[
            jax.ShapeDtypeStruct((bsz, seqlen, 3 * DN_WIDTH), F32),
            jax.ShapeDtypeStruct((bsz, seqlen, DN_AB_PAD), F32),
        ],
        compiler_params=_params("parallel", "parallel"),
        name="dn_feat",
    )(p_dn, p_dn, p_dn, p_dn, conv_w, alog_row, dt_row)


def _dn_scan_kernel(*refs, backward, final):
    if final:
        qkv_ref, gb_ref, s0_ref, z_ref, ob_ref, nw_ref, o_ref, sn_ref, st_ref = refs
    else:
        qkv_ref, gb_ref, s0_ref, o_ref, sn_ref, st_ref = refs
    j = pl.program_id(0)

    @pl.when(j == 0)
    def _():
        st_ref[...] = s0_ref[...]

    bsz, c = qkv_ref.shape[0], qkv_ref.shape[1]
    ri = lax.broadcasted_iota(jnp.int32, (c, c), 0)
    ci = lax.broadcasted_iota(jnp.int32, (c, c), 1)
    incl = (ri <= ci) if backward else (ri >= ci)
    strict = (ri < ci) if backward else (ri > ci)
    tri = incl.astype(F32)
    tri_t = ((ri >= ci) if backward else (ri <= ci)).astype(F32)
    d = 1 if backward else 0
    last = 0 if backward else c - 1
    for b in range(bsz):
        gb = gb_ref[b]
        gc_cols = _dot(tri, gb, HIGHEST)
        gc_rows = _dot(gb.T, tri_t, HIGHEST)
        for hd in range(DN_HEADS):
            col = DN_HEADS * d + hd
            gcc = gc_cols[:, col:col + 1]
            gcr = gc_rows[col:col + 1, :]
            gcl = gc_cols[last:last + 1, col:col + 1]
            beta = gb[:, 2 * DN_HEADS + col:2 * DN_HEADS + col + 1]
            decay = jnp.where(incl, jnp.exp(jnp.where(incl, gcc - gcr, 0.0)), 0.0)
            q = qkv_ref[b, :, hd * DN_HEAD_DIM:(hd + 1) * DN_HEAD_DIM]
            k = qkv_ref[b, :, DN_WIDTH + hd * DN_HEAD_DIM:DN_WIDTH + (hd + 1) * DN_HEAD_DIM]
            v = qkv_ref[b, :, 2 * DN_WIDTH + hd * DN_HEAD_DIM:2 * DN_WIDTH + (hd + 1) * DN_HEAD_DIM]
            kb = k.astype(BF16)
            kq = _dot_nt(jnp.concatenate([kb, q.astype(BF16)], axis=0), kb)
            a = jnp.where(strict, kq[:c] * decay * beta, 0.0)
            attn = jnp.where(incl, kq[c:] * decay, 0.0)
            t_inv = _unit_tri_inverse(a, c)
            e_gc = jnp.exp(gcc)
            wu = _bdot(t_inv, jnp.concatenate([k * (beta * e_gc), v * beta], axis=1))
            s = st_ref[b, hd]
            wq = _bdot(jnp.concatenate([wu[:, :DN_HEAD_DIM], q * e_gc], axis=0), s)
            v_new = wu[:, DN_HEAD_DIM:] - wq[:c]
            o = wq[c:] + _bdot(attn, v_new)
            k_dec = k * jnp.exp(gcl - gcc)
            st_ref[b, hd] = s * jnp.exp(gcl) + _dot_tn(k_dec.astype(BF16), v_new.astype(BF16))
            if final:
                o = o + ob_ref[b, :, hd * DN_HEAD_DIM:(hd + 1) * DN_HEAD_DIM]
                o = o * lax.rsqrt(jnp.mean(o * o, axis=-1, keepdims=True) + NORM_EPS) * nw_ref[...]
                o = o * _silu(z_ref[b, :, hd * DN_HEAD_DIM:(hd + 1) * DN_HEAD_DIM])
            o_ref[b, :, hd * DN_HEAD_DIM:(hd + 1) * DN_HEAD_DIM] = o

    @pl.when(j == pl.num_programs(0) - 1)
    def _():
        sn_ref[...] = st_ref[...]


def _dn_scan(qkv, gb, state0, *, backward, p_dn=None, o_other=None, norm_w=None):
    bsz, seqlen, _ = qkv.shape
    c = DN_CHUNK
    n = seqlen // c
    final = p_dn is not None
    idx = (lambda j: (0, n - 1 - j, 0)) if backward else (lambda j: (0, j, 0))
    st_spec = pl.BlockSpec((bsz, DN_HEADS, DN_HEAD_DIM, DN_HEAD_DIM), lambda j: (0, 0, 0, 0))
    in_specs = [
        pl.BlockSpec((bsz, c, 3 * DN_WIDTH), idx),
        pl.BlockSpec((bsz, c, DN_AB_PAD), idx),
        st_spec,
    ]
    args = [qkv, gb, state0]
    if final:
        zcol = 3 * DN_WIDTH // DN_WIDTH
        in_specs += [
            pl.BlockSpec((bsz, c, DN_WIDTH), lambda j: (0, idx(j)[1], zcol)),
            pl.BlockSpec((bsz, c, DN_WIDTH), idx),
            pl.BlockSpec((1, DN_HEAD_DIM), lambda j: (0, 0)),
        ]
        args += [p_dn, o_other, norm_w.reshape(1, DN_HEAD_DIM)]
    return pl.pallas_call(
        functools.partial(_dn_scan_kernel, backward=backward, final=final),
        grid=(n,),
        in_specs=in_specs,
        out_specs=[pl.BlockSpec((bsz, c, DN_WIDTH), idx), st_spec],
        out_shape=[
            jax.ShapeDtypeStruct((bsz, seqlen, DN_WIDTH), F32),
            jax.ShapeDtypeStruct(state0.shape, F32),
        ],
        scratch_shapes=[pltpu.VMEM(state0.shape, F32)],
        compiler_params=_params("arbitrary"),
        name="dn_scan_bwd" if backward else "dn_scan_fwd",
    )(*args)


def _head_sum(x, ones_bd):
    parts = [_dot(x[:, p * RW_PAIR:(p + 1) * RW_PAIR], ones_bd, HIGHEST) for p in range(RW_PAIRS)]
    return jnp.concatenate(parts, axis=1)


def _pair_ones():
    ri = lax.broadcasted_iota(jnp.int32, (RW_PAIR, RW_PAIR), 0)
    ci = lax.broadcasted_iota(jnp.int32, (RW_PAIR, RW_PAIR), 1)
    return ((ri // RW_HEAD_DIM) == (ci // RW_HEAD_DIM)).astype(F32)


def _rw_feat_kernel(x_ref, xp_ref, xn_ref, mu_ref, w0_ref, w2_ref, a0_ref, a2_ref, g2_ref, kk_ref, ka_ref, u_ref,
                    sh_ref, d0_ref, d1_ref, fin_ref):
    c = pl.program_id(1)
    x = x_ref[0]
    hp = jnp.where(c == 0, 0.0, xp_ref[0])
    hn = jnp.where(c == pl.num_programs(1) - 1, 0.0, xn_ref[0])
    prev, nxt = _shift_rows(x, hp, hn, 1)
    p = x + mu_ref[0:1] * (prev - x) + mu_ref[1:2] * (nxt - x)
    w = RW_WIDTH
    r, k, v = p[:, :w], p[:, w:2 * w], p[:, 2 * w:3 * w]
    lw = p[:, 3 * w:3 * w + RW_LORA]
    la = p[:, 3 * w + RW_LORA:3 * w + 2 * RW_LORA]
    lg = p[:, 3 * w + 2 * RW_LORA:]
    w_log = -_softplus(-(w0_ref[...] + _bdot(jnp.tanh(lw), w2_ref[...]))) - 0.5
    log_decay = -jnp.exp(w_log)
    a = jax.nn.sigmoid(a0_ref[...] + _bdot(la, a2_ref[...]))
    gate = _bdot(jax.nn.sigmoid(lg), g2_ref[...])
    ones_bd = _pair_ones()
    kx = k * kk_ref[...]
    kk = kx * lax.rsqrt(_head_sum(kx * kx, ones_bd) + NORM_EPS)
    sh_ref[0, :, :w] = r
    sh_ref[0, :, w:2 * w] = v
    sh_ref[0, :, 2 * w:] = kk
    bonus = jnp.zeros_like(v)
    for d, d_ref in enumerate((d0_ref, d1_ref)):
        a_d = a[:, d * w:(d + 1) * w]
        k_dir = k * (1.0 + (a_d - 1.0) * ka_ref[...])
        d_ref[0, :, :w] = log_decay[:, d * w:(d + 1) * w]
        d_ref[0, :, w:2 * w] = k_dir
        d_ref[0, :, 2 * w:] = a_d * kk
        bonus = bonus + _head_sum(r * k_dir * u_ref[:, d * w:(d + 1) * w], ones_bd) * v
    fin_ref[0, :, :w] = gate
    fin_ref[0, :, w:] = bonus


def _rw_features(p_rw, n_cols, wts):
    bsz, seqlen, _ = p_rw.shape
    rows = seqlen // n_cols
    pv = p_rw.reshape(bsz, rows, n_cols * RW_P_WIDTH)
    hpb = rows // HALO
    const = lambda shape: pl.BlockSpec(shape, lambda b, c: (0, 0))
    w = RW_WIDTH
    out_spec = lambda width: pl.BlockSpec((1, rows, width), lambda b, c: (b, c, 0))
    return pl.pallas_call(
        _rw_feat_kernel,
        grid=(bsz, n_cols),
        in_specs=[
            pl.BlockSpec((1, rows, RW_P_WIDTH), lambda b, c: (b, 0, c)),
            pl.BlockSpec((1, HALO, RW_P_WIDTH), lambda b, c: (b, hpb - 1, jnp.maximum(c - 1, 0))),
            pl.BlockSpec((1, HALO, RW_P_WIDTH), lambda b, c: (b, 0, jnp.minimum(c + 1, n_cols - 1))),
            const((8, RW_P_WIDTH)),
            const((1, 2 * w)), const((RW_LORA, 2 * w)),
            const((1, 2 * w)), const((RW_LORA, 2 * w)),
            const((RW_LORA, w)),
            const((1, w)), const((1, w)), const((1, 2 * w)),
        ],
        out_specs=[out_spec(3 * w), out_spec(3 * w), out_spec(3 * w), out_spec(2 * w)],
        out_shape=[
            jax.ShapeDtypeStruct((bsz, seqlen, 3 * w), F32),
            jax.ShapeDtypeStruct((bsz, seqlen, 3 * w), F32),
            jax.ShapeDtypeStruct((bsz, seqlen, 3 * w), F32),
            jax.ShapeDtypeStruct((bsz, seqlen, 2 * w), F32),
        ],
        compiler_params=_params("parallel", "parallel"),
        name="rw_feat",
    )(pv, pv, pv, *wts)


def _stack_heads(x, lane_lo):
    return jnp.concatenate([jnp.where(lane_lo, x, 0.0), jnp.where(lane_lo, 0.0, x)], axis=0)


def _rw_scan_kernel(*refs, backward, final):
    if final:
        sh_ref, dr_ref, s0_ref, fin_ref, yb_ref, ln_ref, y_ref, sn_ref, st_ref = refs
    else:
        sh_ref, dr_ref, s0_ref, y_ref, sn_ref, st_ref = refs
    j = pl.program_id(0)

    @pl.when(j == 0)
    def _():
        st_ref[...] = s0_ref[...]

    bsz, c = sh_ref.shape[0], sh_ref.shape[1]
    w = RW_WIDTH
    c2 = 2 * c
    ri = lax.broadcasted_iota(jnp.int32, (c, c), 0)
    ci = lax.broadcasted_iota(jnp.int32, (c, c), 1)
    tri = ((ri <= ci) if backward else (ri >= ci)).astype(F32)
    ri2 = lax.broadcasted_iota(jnp.int32, (c2, c2), 0)
    ci2 = lax.broadcasted_iota(jnp.int32, (c2, c2), 1)
    incl = (ri2 <= ci2) if backward else (ri2 >= ci2)
    strict = (ri2 < ci2) if backward else (ri2 > ci2)
    pi =lax.broadcasted_iota(jnp.int32, (RW_PAIR, RW_PAIR), 0)
    pj = lax.broadcasted_iota(jnp.int32, (RW_PAIR, RW_PAIR), 1)
    lane_lo = lax.broadcasted_iota(jnp.int32, (c, RW_PAIR), 1) < RW_HEAD_DIM
    last = 0 if backward else c - 1
    ones_bd = _pair_ones() if final else None
    for b in range(bsz):
        log_decay = dr_ref[b, :, :w]
        cum = _dot(tri, log_decay, HIGHEST)
        cum_last = cum[last:last + 1, :]
        e_in = jnp.exp(cum)
        e_out = jnp.exp(-cum)
        e_tail = jnp.exp(cum_last - cum)
        r_t = sh_ref[b, :, :w] * e_in
        k_t = sh_ref[b, :, 2 * w:] * jnp.exp(cum - log_decay)
        k_i = dr_ref[b, :, w:2 * w] * e_out
        b_i = dr_ref[b, :, 2 * w:] * e_out
        k_d = dr_ref[b, :, w:2 * w] * e_tail
        b_d = dr_ref[b, :, 2 * w:] * e_tail
        w_last = jnp.exp(cum_last)
        for p in range(RW_PAIRS):
            sl = slice(p * RW_PAIR, (p + 1) * RW_PAIR)
            st = lambda x: _stack_heads(x[:, sl], lane_lo).astype(BF16)
            kts, rts, kis, bis, kds, bds = st(k_t), st(r_t), st(k_i), st(b_i), st(k_d), st(b_d)
            vs = _stack_heads(sh_ref[b, :, w + p * RW_PAIR:w + (p + 1) * RW_PAIR], lane_lo).astype(BF16)
            g = _dot_nt(jnp.concatenate([kts, rts], axis=0), jnp.concatenate([bis, kis], axis=0))
            a_b = jnp.where(strict, g[:c2, :c2], 0.0)
            a_k = jnp.where(strict, g[:c2, c2:], 0.0)
            a_rb = jnp.where(incl, g[c2:, :c2], 0.0)
            a_rk = jnp.where(incl, g[c2:, c2:], 0.0)
            t_inv = _unit_tri_inverse(a_b, c)
            x = _bdot(a_k, vs)
            uk = _bdot(t_inv, jnp.concatenate([x, kts.astype(F32)], axis=1))
            zeros = jnp.zeros((c2, RW_PAIR), BF16)
            rhs = jnp.concatenate([jnp.concatenate([vs, zeros], axis=1), uk.astype(BF16)], axis=0)
            yr = _bdot(jnp.concatenate([a_rk, -a_rb], axis=1), rhs)
            nm = _dot_tn(jnp.concatenate([kds, -bds], axis=0), rhs)
            m_full = nm[:, RW_PAIR:] + jnp.where(pi == pj, w_last[:, sl], 0.0)
            r_hat = rts.astype(F32) + yr[:, RW_PAIR:]
            h0 = st_ref[b, p]
            yh = _bdot(jnp.concatenate([r_hat, m_full], axis=0), h0)
            ys = yh[:c2] + yr[:, :RW_PAIR]
            st_ref[b, p] = yh[c2:] + nm[:, :RW_PAIR]
            y = ys[:c] + ys[c:]
            if final:
                y = y + yb_ref[b, :, sl] + fin_ref[b, :, w + p * RW_PAIR:w + (p + 1) * RW_PAIR]
                mean = _dot(y, ones_bd, HIGHEST) * (1.0 / RW_HEAD_DIM)
                yc = y - mean
                var = _dot(yc * yc, ones_bd, HIGHEST) * (1.0 / RW_HEAD_DIM)
                y = yc * lax.rsqrt(var + RW_GN_EPS) * ln_ref[0:1, sl] + ln_ref[1:2, sl]
                y = y * fin_ref[b, :, sl]
            y_ref[b, :, sl] = y

    @pl.when(j == pl.num_programs(0) - 1)
    def _():
        sn_ref[...] = st_ref[...]


def _rw_scan(shared, per_dir, state0, n_cols, *, backward, fin=None, y_other=None, ln=None):
    bsz, seqlen, _ = shared.shape
    c = RW_CHUNK
    n = seqlen // c
    w = RW_WIDTH
    final = fin is not None
    blk = (lambda j: n - 1 - j) if backward else (lambda j: j)
    idx = lambda j: (0, blk(j), 0)
    st_spec = pl.BlockSpec(state0.shape, lambda j: (0, 0, 0, 0))
    in_specs = [pl.BlockSpec((bsz, c, 3 * w), idx), pl.BlockSpec((bsz, c, 3 * w), idx), st_spec]
    args = [shared, per_dir, state0]
    if final:
        rows = seqlen // n_cols
        cpc = rows // c
        in_specs += [
            pl.BlockSpec((bsz, c, 2 * w), idx),
            pl.BlockSpec((bsz, c, w), idx),
            pl.BlockSpec((2, w), lambda j: (0, 0)),
        ]
        args += [fin, y_other, ln]
        out_shape_y = jax.ShapeDtypeStruct((bsz, rows, n_cols * w), F32)
        y_spec = pl.BlockSpec((bsz, c, w), lambda j: (0, blk(j) % cpc, blk(j) // cpc))
    else:
        out_shape_y = jax.ShapeDtypeStruct((bsz, seqlen, w), F32)
        y_spec = pl.BlockSpec((bsz, c, w), idx)
    y, state = pl.pallas_call(
        functools.partial(_rw_scan_kernel, backward=backward, final=final),
        grid=(n,),
        in_specs=in_specs,
        out_specs=[y_spec, st_spec],
        out_shape=[out_shape_y, jax.ShapeDtypeStruct(state0.shape, F32)],
        scratch_shapes=[pltpu.VMEM(state0.shape, F32)],
        compiler_params=_params("arbitrary"),
        name="rw_scan_bwd" if backward else "rw_scan_fwd",
    )(*args)
    if final:
        y = y.reshape(bsz, seqlen, w)
    return y, state


def _outproj_kernel(h_ref, mod_ref, dn_ref, rw_ref, wdn_ref, wrw_ref, o_ref):
    mix = _dot(dn_ref[0].astype(BF16), wdn_ref[...]) + _dot(rw_ref[0].astype(BF16), wrw_ref[...])
    o_ref[0] = h_ref[0] + mod_ref[0, 5:6, :] * mix


def _outproj(h, mod, dn, rw, w_out, *, tm):
    bsz, seqlen, d = h.shape
    tok = lambda width: pl.BlockSpec((1, tm, width), lambda b, i: (b, i, 0))
    return pl.pallas_call(
        _outproj_kernel,
        grid=(bsz, seqlen // tm),
        in_specs=[
            tok(d),
            pl.BlockSpec((1, N_MOD, d), lambda b, i: (b, 0, 0)),
            tok(DN_WIDTH), tok(RW_WIDTH),
            pl.BlockSpec((DN_WIDTH, d), lambda b, i: (0, 0)),
            pl.BlockSpec((RW_WIDTH, d), lambda b, i: (1, 0)),
        ],
        out_specs=tok(d),
        out_shape=jax.ShapeDtypeStruct(h.shape, F32),
        compiler_params=_params("parallel", "parallel"),
        name="outproj",
    )(h, mod, dn, rw, w_out, w_out)


def _final_norm_kernel(h_ref, g_ref, o_ref):
    h = h_ref[0]
    o_ref[0] = h * lax.rsqrt(jnp.mean(h * h, axis=-1, keepdims=True) + NORM_EPS) * g_ref[...]


def _final_norm(h, gain, *, tm):
    bsz, seqlen, d = h.shape
    return pl.pallas_call(
        _final_norm_kernel,
        grid=(bsz, seqlen // tm),
        in_specs=[pl.BlockSpec((1, tm, d), lambda b, i: (b, i, 0)), pl.BlockSpec((1, d), lambda b, i: (0, 0))],
        out_specs=pl.BlockSpec((1, tm, d), lambda b, i: (b, i, 0)),
        out_shape=jax.ShapeDtypeStruct(h.shape, F32),
        compiler_params=_params("parallel", "parallel"),
        name="final_norm",
    )(h, gain.reshape(1, d))


def _pad_rows(x, rows):
    return jnp.pad(x, ((0, rows - x.shape[0]), (0, 0)))


def _block_diag2(m):
    z = jnp.zeros_like(m[0])
    return jnp.concatenate([jnp.concatenate([m[0], z], axis=1), jnp.concatenate([z, m[1]], axis=1)], axis=0)


def _token_mixing(h_ctx, h_lat, mod_ctx, mod_lat, gain, lw, *, need_ctx_out, tm_lat, tm_ctx):
    bsz = h_lat.shape[0]
    n_cols = GRID_W
    outs = {}
    dn_state = jnp.zeros((2, bsz, DN_HEADS, DN_HEAD_DIM, DN_HEAD_DIM), F32)
    rw_state = jnp.zeros((2, bsz, RW_PAIRS, RW_PAIR, RW_PAIR), F32)
    dn_states = [dn_state[0], dn_state[1]]
    rw_states = [rw_state[0], rw_state[1]]
    for name, h, mod, tm, cols in (("ctx", h_ctx, mod_ctx, tm_ctx, 1), ("lat", h_lat, mod_lat, tm_lat, n_cols)):
        p_dn, p_rw = _proj(h, mod, gain, lw["w_dn"], lw["w_rw"], tm=min(tm, 512))
        qkv, gb = _dn_features(p_dn, lw["conv_w"], lw["alog_row"], lw["dt_row"], rows=min(512, h.shape[1]))
        o_b, dn_states[1] = _dn_scan(qkv, gb, dn_states[1], backward=True)
        dn_out, dn_states[0] = _dn_scan(qkv, gb, dn_states[0], backward=False, p_dn=p_dn, o_other=o_b,
                                        norm_w=lw["dn_norm"])
        shared, dir0, dir1, fin = _rw_features(p_rw, cols, lw["rw_feat"])
        y_b, rw_states[1] = _rw_scan(shared, dir1, rw_states[1], cols, backward=True)
        rw_out, rw_states[0] = _rw_scan(shared, dir0, rw_states[0], cols, backward=False, fin=fin, y_other=y_b,
                                        ln=lw["rw_ln"])
        if name == "lat" or need_ctx_out:
            outs[name] = _outproj(h, mod, dn_out, rw_out, lw["w_out"], tm=tm)
    return outs.get("ctx"), outs["lat"]


def kernel(x, c, ctx, c_ctx, ada_w, ada_b, norm_g, ffn_w13, ffn_w2, w_in, dn_conv, dn_a_log, dn_dt_bias, dn_norm,
           rw_mu, rw_w0, rw_w2, rw_a0, rw_a2, rw_g2, rw_kk, rw_ka, rw_u, rw_ln, w_out, final_norm):
    depth = ada_w.shape[0]
    bsz, seqlen, d = x.shape
    ctx_len = ctx.shape[1]
    tm_lat, tm_ctx = 1024, ctx_len

    s_rows = jnp.concatenate([c_ctx[None, :], c, jnp.zeros((8 - 1 - bsz, d), F32)], axis=0)
    mods = _adaln(s_rows, ada_w, ada_b).reshape(depth, 8, N_MOD, d)

    w13_bf, w2_bf = ffn_w13.astype(BF16), ffn_w2.astype(BF16)
    dn_cols = 4 * DN_WIDTH + 4 * DN_HEADS
    w_dn = jnp.pad(w_in[:, :, :dn_cols], ((0, 0), (0, 0), (0, DN_P_WIDTH - dn_cols))).astype(BF16)
    w_rw = w_in[:, :, dn_cols:].astype(BF16)
    w_out_bf = w_out.astype(BF16)
    pad_lanes = lambda v: jnp.pad(v.reshape(1, -1), ((0, 0), (0, DN_AB_PAD - v.size)))

    h_lat, h_ctx = x, ctx
    for i in range(depth):
        last = i == depth - 1
        mod_ctx = jnp.broadcast_to(mods[i, 0:1], (bsz, N_MOD, d))
        mod_lat = mods[i, 1:1 + bsz]
        lw = {
            "w_dn": w_dn[i], "w_rw": w_rw[i], "w_out": w_out_bf[i],
            "conv_w": _pad_rows(dn_conv[i], 8),
            "alog_row": pad_lanes(dn_a_log[i]), "dt_row": pad_lanes(dn_dt_bias[i]),
            "dn_norm": dn_norm[i],
            "rw_feat": (
                _pad_rows(rw_mu[i], 8),
                rw_w0[i].reshape(1, -1), _block_diag2(rw_w2[i]).astype(BF16),
                rw_a0[i].reshape(1, -1), _block_diag2(rw_a2[i]).astype(BF16),
                rw_g2[i].astype(BF16),
                rw_kk[i].reshape(1, -1), rw_ka[i].reshape(1, -1), rw_u[i].reshape(1, -1),
            ),
            "rw_ln": rw_ln[i],
        }
        h_lat = _ffn(h_lat, mod_lat, norm_g[i, 0], w13_bf[i, 0], w2_bf[i, 0], row0=0, tm=tm_lat)
        h_ctx = _ffn(h_ctx, mod_ctx, norm_g[i, 0], w13_bf[i, 0], w2_bf[i, 0], row0=0, tm=tm_ctx)
        h_ctx_mix, h_lat = _token_mixing(h_ctx, h_lat, mod_ctx, mod_lat, norm_g[i, 1], lw,
                                         need_ctx_out=not last, tm_lat=tm_lat, tm_ctx=tm_ctx)
        h_lat = _ffn(h_lat, mod_lat, norm_g[i, 2], w13_bf[i, 1], w2_bf[i, 1], row0=6, tm=tm_lat)
        if not last:
            h_ctx = _ffn(h_ctx_mix, mod_ctx, norm_g[i, 2], w13_bf[i, 1], w2_bf[i, 1], row0=6, tm=tm_ctx)
    return _final_norm(h_lat, final_norm, tm=tm_lat)
```

```python
import functools

import jax
import jax.numpy as jnp
from jax import lax
from jax.experimental import pallas as pl
from jax.experimental.pallas import tpu as pltpu

F32 = jnp.float32
BF16 = jnp.bfloat16
HIGHEST = lax.Precision.HIGHEST

NORM_EPS = 1e-6
RW_GN_EPS = 64e-5
N_MOD = 9
GRID_W = 64
SHORT_CONV = 5

DN_HEADS = 4
DN_HEAD_DIM = 128
DN_WIDTH = DN_HEADS * DN_HEAD_DIM
DN_AB_PAD = 128
DN_P_WIDTH = 4 * DN_WIDTH + DN_AB_PAD
DN_CHUNK = 128

RW_HEADS = 8
RW_HEAD_DIM = 64
RW_WIDTH = RW_HEADS * RW_HEAD_DIM
RW_LORA = 128
RW_P_WIDTH = 3 * RW_WIDTH + 3 * RW_LORA
RW_CHUNK = 64
RW_PAIR = 2 * RW_HEAD_DIM
RW_PAIRS = RW_WIDTH // RW_PAIR

HALO = 8
V7X_VMEM_LIMIT = 56 * 1024 * 1024


def _params(*sem):
    return pltpu.CompilerParams(dimension_semantics=sem, vmem_limit_bytes=V7X_VMEM_LIMIT)


def _dot(a, b, precision=None):
    return jnp.dot(a, b, preferred_element_type=F32, precision=precision)


def _dot_nt(a, b):
    return lax.dot_general(a, b, (((1,), (1,)), ((), ())), preferred_element_type=F32)


def _dot_tn(a, b):
    return lax.dot_general(a, b, (((0,), (0,)), ((), ())), preferred_element_type=F32)


def _bdot(a, b):
    return _dot(a.astype(BF16), b.astype(BF16))


def _silu(x):
    return x * jax.nn.sigmoid(x)


def _softplus(x):
    return jnp.maximum(x, 0.0) + jnp.log1p(jnp.exp(-jnp.abs(x)))


def _rms_modulate(h, gain, shift, scale):
    y = h * lax.rsqrt(jnp.mean(h * h, axis=-1, keepdims=True) + NORM_EPS) * gain
    return y * (1.0 + scale) + shift


def _unit_tri_inverse(a, size):
    n = a.shape[0]
    diff = lax.broadcasted_iota(jnp.int32, (n, n), 0) ^ lax.broadcasted_iota(jnp.int32, (n, n), 1)
    t = jnp.where(diff == 0, 1.0, jnp.where(diff == 1, -a, 0.0))
    level = 1
    while (1 << level) < size:
        l = jnp.where((diff >> level) == 1, a, 0.0)
        tl = _bdot(t, l)
        yield
        t = t - _bdot(tl, t)
        yield
        level += 1
    return t


def _run_lockstep(gens):
    results = [None] * len(gens)
    live = list(enumerate(gens))
    while live:
        still = []
        for i, g in live:
            try:
                next(g)
                still.append((i, g))
            except StopIteration as done:
                results[i] = done.value
        live = still
    return results


def _shift_rows(x, halo_prev, halo_next, s):
    rows = x.shape[0]
    row8 = lax.broadcasted_iota(jnp.int32, (HALO, x.shape[1]), 0)
    back = pltpu.roll(x, s, 0)
    hp = pltpu.roll(halo_prev, s, 0)
    back = jnp.concatenate([jnp.where(row8 < s, hp, back[:HALO]), back[HALO:]], axis=0)
    fwd = pltpu.roll(x, rows - s, 0)
    hn = pltpu.roll(halo_next, HALO - s, 0)
    fwd = jnp.concatenate([fwd[:rows - HALO], jnp.where(row8 >= HALO - s, hn, fwd[rows - HALO:])], axis=0)
    return back, fwd


def _adaln_kernel(s_ref, w_ref, b_ref, o_ref):
    s = _silu(s_ref[...])
    o_ref[0] = _dot(s, w_ref[0], HIGHEST) + b_ref[0]


def _adaln(s_rows, ada_w, ada_b):
    depth, d, n = ada_w.shape
    tn = n // 8
    return pl.pallas_call(
        _adaln_kernel,
        grid=(depth, n // tn),
        in_specs=[
            pl.BlockSpec((8, d), lambda l, j: (0, 0)),
            pl.BlockSpec((1, d, tn), lambda l, j: (l, 0, j)),
            pl.BlockSpec((1, 1, tn), lambda l, j: (l, 0, j)),
        ],
        out_specs=pl.BlockSpec((1, 8, tn), lambda l, j: (l, 0, j)),
        out_shape=jax.ShapeDtypeStruct((depth, 8, n), F32),
        compiler_params=_params("parallel", "parallel"),
        name="adaln",
    )(s_rows, ada_w, ada_b.reshape(depth, 1, n))


def _ffn_kernel(h_ref, mod_ref, g_ref, w1_ref, w3_ref, w2_ref, o_ref, xn_ref, acc_ref, *, row0):
    k = pl.program_id(2)

    @pl.when(k == 0)
    def _():
        xn = _rms_modulate(h_ref[0], g_ref[...], mod_ref[0, row0:row0 + 1, :], mod_ref[0, row0 + 1:row0 + 2, :])
        xn_ref[...] = xn.astype(BF16)
        acc_ref[...] = jnp.zeros_like(acc_ref)

    xn = xn_ref[...]
    gate = _dot(xn, w1_ref[...])
    up = _dot(xn, w3_ref[...])
    acc_ref[...] += _dot((_silu(gate) * up).astype(BF16), w2_ref[...])

    @pl.when(k == pl.num_programs(2) - 1)
    def _():
        o_ref[0] = h_ref[0] + 0.5 * mod_ref[0, row0 + 2:row0 + 3, :] * acc_ref[...]


def _ffn(h, mod, gain, w13, w2, *, row0, tm):
    bsz, seqlen, d = h.shape
    dff = w2.shape[0]
    tf = dff // 2
    nk = dff // tf
    return pl.pallas_call(
        functools.partial(_ffn_kernel, row0=row0),
        grid=(bsz, seqlen // tm, nk),
        in_specs=[
            pl.BlockSpec((1, tm, d), lambda b, i, k: (b, i, 0)),
            pl.BlockSpec((1, N_MOD, d), lambda b, i, k: (b, 0, 0)),
            pl.BlockSpec((1, d), lambda b, i, k: (0, 0)),
            pl.BlockSpec((d, tf), lambda b, i, k: (0, k)),
            pl.BlockSpec((d, tf), lambda b, i, k: (0, k + nk)),
            pl.BlockSpec((tf, d), lambda b, i, k: (k, 0)),
        ],
        out_specs=pl.BlockSpec((1, tm, d), lambda b, i, k: (b, i, 0)),
        out_shape=jax.ShapeDtypeStruct(h.shape, F32),
        scratch_shapes=[pltpu.VMEM((tm, d), BF16), pltpu.VMEM((tm, d), F32)],
        compiler_params=_params("parallel", "parallel", "arbitrary"),
        name="ffn",
    )(h, mod, gain.reshape(1, d), w13, w13, w2)


def _proj_kernel(h_ref, mod_ref, g_ref, wdn_ref, wrw_ref, pdn_ref, prw_ref):
    xn = _rms_modulate(h_ref[0], g_ref[...], mod_ref[0, 3:4, :], mod_ref[0, 4:5, :]).astype(BF16)
    pdn_ref[0] = _dot(xn, wdn_ref[...])
    prw_ref[0] = _dot(xn, wrw_ref[...])


def _proj(h, mod, gain, w_dn, w_rw, *, tm):
    bsz, seqlen, d = h.shape
    return pl.pallas_call(
        _proj_kernel,
        grid=(bsz, seqlen // tm),
        in_specs=[
            pl.BlockSpec((1, tm, d), lambda b, i: (b, i, 0)),
            pl.BlockSpec((1, N_MOD, d), lambda b, i: (b, 0, 0)),
            pl.BlockSpec((1, d), lambda b, i: (0, 0)),
            pl.BlockSpec((d, DN_P_WIDTH), lambda b, i: (0, 0)),
            pl.BlockSpec((d, RW_P_WIDTH), lambda b, i: (0, 0)),
        ],
        out_specs=[
            pl.BlockSpec((1, tm, DN_P_WIDTH), lambda b, i: (b, i, 0)),
            pl.BlockSpec((1, tm, RW_P_WIDTH), lambda b, i: (b, i, 0)),
        ],
        out_shape=[
            jax.ShapeDtypeStruct((bsz, seqlen, DN_P_WIDTH), F32),
            jax.ShapeDtypeStruct((bsz, seqlen, RW_P_WIDTH), F32),
        ],
        compiler_params=_params("parallel", "parallel"),
        name="proj",
    )(h, mod, gain.reshape(1, d), w_dn, w_rw)


def _dn_feat_kernel(x_ref, xp_ref, xn_ref, ab_ref, cw_ref, alog_ref, dt_ref, qkv_ref, gb_ref):
    i = pl.program_id(1)
    x = x_ref[0]
    hp = jnp.where(i == 0, 0.0, xp_ref[0])
    hn = jnp.where(i == pl.num_programs(1) - 1, 0.0, xn_ref[0])
    b1, f1 = _shift_rows(x, hp, hn, 1)
    b2, f2 = _shift_rows(x, hp, hn, 2)
    conv = cw_ref[0:1] * b2 + cw_ref[1:2] * b1 + cw_ref[2:3] * x + cw_ref[3:4] * f1 + cw_ref[4:5] * f2
    act = _silu(conv)
    for hd in range(2 * DN_HEADS):
        t = act[:, hd * DN_HEAD_DIM:(hd + 1) * DN_HEAD_DIM]
        t = t * lax.rsqrt(jnp.sum(t * t, axis=-1, keepdims=True) + NORM_EPS)
        if hd < DN_HEADS:
            t = t * DN_HEAD_DIM ** -0.5
        qkv_ref[0, :, hd * DN_HEAD_DIM:(hd + 1) * DN_HEAD_DIM] = t
    qkv_ref[0, :, 2 * DN_WIDTH:] = act[:, 2 * DN_WIDTH:]
    ab = ab_ref[0]
    lane = lax.broadcasted_iota(jnp.int32, ab.shape, 1)
    g = -jnp.exp(alog_ref[...]) * _softplus(ab + dt_ref[...])
    gb_ref[0] = jnp.where(lane < 2 * DN_HEADS, g, jax.nn.sigmoid(ab))


def _dn_features(p_dn, conv_w, alog_row, dt_row, *, rows):
    bsz, seqlen, _ = p_dn.shape
    nblk = seqlen // rows
    hpb = rows // HALO
    return pl.pallas_call(
        _dn_feat_kernel,
        grid=(bsz, nblk),
        in_specs=[
            pl.BlockSpec((1, rows, 3 * DN_WIDTH), lambda b, i: (b, i, 0)),
            pl.BlockSpec((1, HALO, 3 * DN_WIDTH), lambda b, i: (b, jnp.maximum(i * hpb - 1, 0), 0)),
            pl.BlockSpec((1, HALO, 3 * DN_WIDTH), lambda b, i: (b, jnp.minimum((i + 1) * hpb, nblk * hpb - 1), 0)),
            pl.BlockSpec((1, rows, DN_AB_PAD), lambda b, i: (b, i, 4 * DN_WIDTH // DN_AB_PAD)),
            pl.BlockSpec((8, 3 * DN_WIDTH), lambda b, i: (0, 0)),
            pl.BlockSpec((1, DN_AB_PAD), lambda b, i: (0, 0)),
            pl.BlockSpec((1, DN_AB_PAD), lambda b, i: (0, 0)),
        ],
        out_specs=[
            pl.BlockSpec((1, rows, 3 * DN_WIDTH), lambda b, i: (b, i, 0)),
            pl.BlockSpec((1, rows, DN_AB_PAD), lambda b, i: (b, i, 0)),
        ],
        out_shape=[
            jax.ShapeDtypeStruct((bsz, seqlen, 3 * DN_WIDTH), F32),
            jax.ShapeDtypeStruct((bsz, seqlen, DN_AB_PAD), F32),
        ],
        compiler_params=_params("parallel", "parallel"),
        name="dn_feat",
    )(p_dn, p_dn, p_dn, p_dn, conv_w, alog_row, dt_row)


def _dn_scan_kernel(*refs, backward, final):
    if final:
        qkv_ref, gb_ref, s0_ref, z_ref, ob_ref, nw_ref, o_ref, sn_ref, st_ref = refs
    else:
        qkv_ref, gb_ref, s0_ref, o_ref, sn_ref, st_ref = refs
    j = pl.program_id(0)

    @pl.when(j == 0)
    def _():
        st_ref[...] = s0_ref[...]

    bsz, c = qkv_ref.shape[0], qkv_ref.shape[1]
    ri = lax.broadcasted_iota(jnp.int32, (c, c), 0)
    ci = lax.broadcasted_iota(jnp.int32, (c, c), 1)
    incl = (ri <= ci) if backward else (ri >= ci)
    strict = (ri < ci) if backward else (ri > ci)
    tri = incl.astype(F32)
    tri_t = ((ri >= ci) if backward else (ri <= ci)).astype(F32)
    d = 1 if backward else 0
    last = 0 if backward else c - 1

    def chunk(q, k, v, beta, gcc, gcr, gcl, s):
        decay = jnp.where(incl, jnp.exp(jnp.where(incl, gcc - gcr, 0.0)), 0.0)
        kb = k.astype(BF16)
        kq = _dot_nt(jnp.concatenate([kb, q.astype(BF16)], axis=0), kb)
        yield
        a = jnp.where(strict, kq[:c] * decay * beta, 0.0)
        attn = jnp.where(incl, kq[c:] * decay, 0.0)
        t_inv = yield from _unit_tri_inverse(a, c)
        e_gc = jnp.exp(gcc)
        wu = _bdot(t_inv, jnp.concatenate([k * (beta * e_gc), v * beta], axis=1))
        yield
        wq = _bdot(jnp.concatenate([wu[:, :DN_HEAD_DIM], q * e_gc], axis=0), s)
        yield
        v_new = wu[:, DN_HEAD_DIM:] - wq[:c]
        o = wq[c:] + _bdot(attn, v_new)
        k_dec = k * jnp.exp(gcl - gcc)
        s_new = s * jnp.exp(gcl) + _dot_tn(k_dec.astype(BF16), v_new.astype(BF16))
        return o, s_new

    heads = [(b, hd) for b in range(bsz) for hd in range(DN_HEADS)]
    head_cols = lambda base, hd: slice(base + hd * DN_HEAD_DIM, base + (hd + 1) * DN_HEAD_DIM)
    gbs = [gb_ref[b] for b in range(bsz)]
    gc_cols = [_dot(tri, gb, HIGHEST) for gb in gbs]
    gc_rows = [_dot(gb.T, tri_t, HIGHEST) for gb in gbs]
    gens = []
    for b, hd in heads:
        col = DN_HEADS * d + hd
        gens.append(chunk(
            qkv_ref[b, :, head_cols(0, hd)], qkv_ref[b, :, head_cols(DN_WIDTH, hd)],
            qkv_ref[b, :, head_cols(2 * DN_WIDTH, hd)],
            gbs[b][:, 2 * DN_HEADS + col:2 * DN_HEADS + col + 1],
            gc_cols[b][:, col:col + 1], gc_rows[b][col:col + 1, :], gc_cols[b][last:last + 1, col:col + 1],
            st_ref[b, hd]))
    results = _run_lockstep(gens)
    for (b, hd), (o, s_new) in zip(heads, results):
        st_ref[b, hd] = s_new
        if final:
            o = o + ob_ref[b, :, head_cols(0, hd)]
            o = o * lax.rsqrt(jnp.mean(o * o, axis=-1, keepdims=True) + NORM_EPS) * nw_ref[...]
            o = o * _silu(z_ref[b, :, head_cols(0, hd)])
        o_ref[b, :, head_cols(0, hd)] = o

    @pl.when(j == pl.num_programs(0) - 1)
    def _():
        sn_ref[...] = st_ref[...]


def _dn_scan(qkv, gb, state0, *, backward, p_dn=None, o_other=None, norm_w=None):
    bsz, seqlen, _ = qkv.shape
    c = DN_CHUNK
    n = seqlen // c
    final = p_dn is not None
    idx = (lambda j: (0, n - 1 - j, 0)) if backward else (lambda j: (0, j, 0))
    st_spec = pl.BlockSpec((bsz, DN_HEADS, DN_HEAD_DIM, DN_HEAD_DIM), lambda j: (0, 0, 0, 0))
    in_specs = [
        pl.BlockSpec((bsz, c, 3 * DN_WIDTH), idx),
        pl.BlockSpec((bsz, c, DN_AB_PAD), idx),
        st_spec,
    ]
    args = [qkv, gb, state0]
    if final:
        zcol = 3 * DN_WIDTH // DN_WIDTH
        in_specs += [
            pl.BlockSpec((bsz, c, DN_WIDTH), lambda j: (0, idx(j)[1], zcol)),
            pl.BlockSpec((bsz, c, DN_WIDTH), idx),
            pl.BlockSpec((1, DN_HEAD_DIM), lambda j: (0, 0)),
        ]
        args += [p_dn, o_other, norm_w.reshape(1, DN_HEAD_DIM)]
    return pl.pallas_call(
        functools.partial(_dn_scan_kernel, backward=backward, final=final),
        grid=(n,),
        in_specs=in_specs,
        out_specs=[pl.BlockSpec((bsz, c, DN_WIDTH), idx), st_spec],
        out_shape=[
            jax.ShapeDtypeStruct((bsz, seqlen, DN_WIDTH), F32),
            jax.ShapeDtypeStruct(state0.shape, F32),
        ],
        scratch_shapes=[pltpu.VMEM(state0.shape, F32)],
        compiler_params=_params("arbitrary"),
        name="dn_scan_bwd" if backward else "dn_scan_fwd",
    )(*args)


def _head_sum(x, ones_bd):
    parts = [_dot(x[:, p * RW_PAIR:(p + 1) * RW_PAIR], ones_bd, HIGHEST) for p in range(RW_PAIRS)]
    return jnp.concatenate(parts, axis=1)


def _pair_ones():
    ri = lax.broadcasted_iota(jnp.int32, (RW_PAIR, RW_PAIR), 0)
    ci = lax.broadcasted_iota(jnp.int32, (RW_PAIR, RW_PAIR), 1)
    return ((ri // RW_HEAD_DIM) == (ci // RW_HEAD_DIM)).astype(F32)


def _rw_feat_kernel(x_ref, xp_ref, xn_ref, mu_ref, w0_ref, w2_ref, a0_ref, a2_ref, g2_ref, kk_ref, ka_ref, u_ref,
                    sh_ref, d0_ref, d1_ref, fin_ref):
    c = pl.program_id(1)
    x = x_ref[0]
    hp = jnp.where(c == 0, 0.0, xp_ref[0])
    hn = jnp.where(c == pl.num_programs(1) - 1, 0.0, xn_ref[0])
    prev, nxt = _shift_rows(x, hp, hn, 1)
    p = x + mu_ref[0:1] * (prev - x) + mu_ref[1:2] * (nxt - x)
    w = RW_WIDTH
    r, k, v = p[:, :w], p[:, w:2 * w], p[:, 2 * w:3 * w]
    lw = p[:, 3 * w:3 * w + RW_LORA]
    la = p[:, 3 * w + RW_LORA:3 * w + 2 * RW_LORA]
    lg = p[:, 3 * w + 2 * RW_LORA:]
    w_log = -_softplus(-(w0_ref[...] + _bdot(jnp.tanh(lw), w2_ref[...]))) - 0.5
    log_decay = -jnp.exp(w_log)
    a = jax.nn.sigmoid(a0_ref[...] + _bdot(la, a2_ref[...]))
    gate = _bdot(jax.nn.sigmoid(lg), g2_ref[...])
    ones_bd = _pair_ones()
    kx = k * kk_ref[...]
    kk = kx * lax.rsqrt(_head_sum(kx * kx, ones_bd) + NORM_EPS)
    sh_ref[0, :, :w] = r
    sh_ref[0, :, w:2 * w] = v
    sh_ref[0, :, 2 * w:] = kk
    bonus = jnp.zeros_like(v)
    for d, d_ref in enumerate((d0_ref, d1_ref)):
        a_d = a[:, d * w:(d + 1) * w]
        k_dir = k * (1.0 + (a_d - 1.0) * ka_ref[...])
        d_ref[0, :, :w] = log_decay[:, d * w:(d + 1) * w]
        d_ref[0, :, w:2 * w] = k_dir
        d_ref[0, :, 2 * w:] = a_d * kk
        bonus = bonus + _head_sum(r * k_dir * u_ref[:, d * w:(d + 1) * w], ones_bd) * v
    fin_ref[0, :, :w] = gate
    fin_ref[0, :, w:] = bonus


def _rw_features(p_rw, n_cols, wts):
    bsz, seqlen, _ = p_rw.shape
    rows = seqlen // n_cols
    pv = p_rw.reshape(bsz, rows, n_cols * RW_P_WIDTH)
    hpb = rows // HALO
    const = lambda shape: pl.BlockSpec(shape, lambda b, c: (0, 0))
    w = RW_WIDTH
    out_spec = lambda width: pl.BlockSpec((1, rows, width), lambda b, c: (b, c, 0))
    return pl.pallas_call(
        _rw_feat_kernel,
        grid=(bsz, n_cols),
        in_specs=[
            pl.BlockSpec((1, rows, RW_P_WIDTH), lambda b, c: (b, 0, c)),
            pl.BlockSpec((1, HALO, RW_P_WIDTH), lambda b, c: (b, hpb - 1, jnp.maximum(c - 1, 0))),
            pl.BlockSpec((1, HALO, RW_P_WIDTH), lambda b, c: (b, 0, jnp.minimum(c + 1, n_cols - 1))),
            const((8, RW_P_WIDTH)),
            const((1, 2 * w)), const((RW_LORA, 2 * w)),
            const((1, 2 * w)), const((RW_LORA, 2 * w)),
            const((RW_LORA, w)),
            const((1, w)), const((1, w)), const((1, 2 * w)),
        ],
        out_specs=[out_spec(3 * w), out_spec(3 * w), out_spec(3 * w), out_spec(2 * w)],
        out_shape=[
            jax.ShapeDtypeStruct((bsz, seqlen, 3 * w), F32),
            jax.ShapeDtypeStruct((bsz, seqlen, 3 * w), F32),
            jax.ShapeDtypeStruct((bsz, seqlen, 3 * w), F32),
            jax.ShapeDtypeStruct((bsz, seqlen, 2 * w), F32),
        ],
        compiler_params=_params("parallel", "parallel"),
        name="rw_feat",
    )(pv, pv, pv, *wts)


def _stack_heads(x, lane_lo):
    return jnp.concatenate([jnp.where(lane_lo, x, 0.0), jnp.where(lane_lo, 0.0, x)], axis=0)


def _rw_scan_kernel(*refs, backward, final):
    if final:
        sh_ref, dr_ref, s0_ref, fin_ref, yb_ref, ln_ref, y_ref, sn_ref, st_ref = refs
    else:
        sh_ref, dr_ref, s0_ref, y_ref, sn_ref, st_ref = refs
    j = pl.program_id(0)

    @pl.when(j == 0)
    def _():
        st_ref[...] = s0_ref[...]

    bsz, c = sh_ref.shape[0], sh_ref.shape[1]
    w = RW_WIDTH
    c2 = 2 * c
    ri = lax.broadcasted_iota(jnp.int32, (c, c), 0)
    ci = lax.broadcasted_iota(jnp.int32, (c, c), 1)
    tri = ((ri <= ci) if backward else (ri >= ci)).astype(F32)
    ri2 = lax.broadcasted_iota(jnp.int32, (c2, c2), 0)
    ci2 = lax.broadcasted_iota(jnp.int32, (c2, c2), 1)
    incl = (ri2 <= ci2) if backward else (ri2 >= ci2)
    strict = (ri2 < ci2) if backward else (ri2 > ci2)
    pi =lax.broadcasted_iota(jnp.int32, (RW_PAIR, RW_PAIR), 0)
    pj = lax.broadcasted_iota(jnp.int32, (RW_PAIR, RW_PAIR), 1)
    lane_lo = lax.broadcasted_iota(jnp.int32, (c, RW_PAIR), 1) < RW_HEAD_DIM
    last = 0 if backward else c - 1
    ones_bd = _pair_ones() if final else None

    def pair_chunk(kts, rts, kis, bis, kds, bds, vs, w_last, h0):
        g = _dot_nt(jnp.concatenate([kts, rts], axis=0), jnp.concatenate([bis, kis], axis=0))
        yield
        a_b = jnp.where(strict, g[:c2, :c2], 0.0)
        a_k = jnp.where(strict, g[:c2, c2:], 0.0)
        a_rb = jnp.where(incl, g[c2:, :c2], 0.0)
        a_rk = jnp.where(incl, g[c2:, c2:], 0.0)
        x = _bdot(a_k, vs)
        t_inv = yield from _unit_tri_inverse(a_b, c)
        uk = _bdot(t_inv, jnp.concatenate([x, kts.astype(F32)], axis=1))
        yield
        zeros = jnp.zeros((c2, RW_PAIR), BF16)
        rhs = jnp.concatenate([jnp.concatenate([vs, zeros], axis=1), uk.astype(BF16)], axis=0)
        yr = _bdot(jnp.concatenate([a_rk, -a_rb], axis=1), rhs)
        nm = _dot_tn(jnp.concatenate([kds, -bds], axis=0), rhs)
        yield
        m_full = nm[:, RW_PAIR:] + jnp.where(pi == pj, w_last, 0.0)
        r_hat = rts.astype(F32) + yr[:, RW_PAIR:]
        yh = _bdot(jnp.concatenate([r_hat, m_full], axis=0), h0)
        ys = yh[:c2] + yr[:, :RW_PAIR]
        return ys[:c] + ys[c:], yh[c2:] + nm[:, :RW_PAIR]

    pairs, gens = [], []
    for b in range(bsz):
        log_decay = dr_ref[b, :, :w]
        cum = _dot(tri, log_decay, HIGHEST)
        cum_last = cum[last:last + 1, :]
        e_in = jnp.exp(cum)
        e_out = jnp.exp(-cum)
        e_tail = jnp.exp(cum_last - cum)
        r_t = sh_ref[b, :, :w] * e_in
        k_t = sh_ref[b, :, 2 * w:] * jnp.exp(cum - log_decay)
        k_i = dr_ref[b, :, w:2 * w] * e_out
        b_i = dr_ref[b, :, 2 * w:] * e_out
        k_d = dr_ref[b, :, w:2 * w] * e_tail
        b_d = dr_ref[b, :, 2 * w:] * e_tail
        w_last = jnp.exp(cum_last)
        for p in range(RW_PAIRS):
            sl = slice(p * RW_PAIR, (p + 1) * RW_PAIR)
            st = lambda x: _stack_heads(x[:, sl], lane_lo).astype(BF16)
            pairs.append((b, p))
            gens.append(pair_chunk(st(k_t), st(r_t), st(k_i), st(b_i), st(k_d), st(b_d),
                                   st(sh_ref[b, :, w:2 * w]), w_last[:, sl], st_ref[b, p]))
    results = _run_lockstep(gens)
    for (b, p), (y, h_new) in zip(pairs, results):
        sl = slice(p * RW_PAIR, (p + 1) * RW_PAIR)
        st_ref[b, p] = h_new
        if final:
            y = y + yb_ref[b, :, sl] + fin_ref[b, :, w + p * RW_PAIR:w + (p + 1) * RW_PAIR]
            mean = _dot(y, ones_bd, HIGHEST) * (1.0 / RW_HEAD_DIM)
            yc = y - mean
            var = _dot(yc * yc, ones_bd, HIGHEST) * (1.0 / RW_HEAD_DIM)
            y = yc * lax.rsqrt(var + RW_GN_EPS) * ln_ref[0:1, sl] + ln_ref[1:2, sl]
            y = y * fin_ref[b, :, sl]
        y_ref[b, :, sl] = y

    @pl.when(j == pl.num_programs(0) - 1)
    def _():
        sn_ref[...] = st_ref[...]


def _rw_scan(shared, per_dir, state0, n_cols, *, backward, fin=None, y_other=None, ln=None):
    bsz, seqlen, _ = shared.shape
    c = RW_CHUNK
    n = seqlen // c
    w = RW_WIDTH
    final = fin is not None
    blk = (lambda j: n - 1 - j) if backward else (lambda j: j)
    idx = lambda j: (0, blk(j), 0)
    st_spec = pl.BlockSpec(state0.shape, lambda j: (0, 0, 0, 0))
    in_specs = [pl.BlockSpec((bsz, c, 3 * w), idx), pl.BlockSpec((bsz, c, 3 * w), idx), st_spec]
    args = [shared, per_dir, state0]
    if final:
        rows = seqlen // n_cols
        cpc = rows // c
        in_specs += [
            pl.BlockSpec((bsz, c, 2 * w), idx),
            pl.BlockSpec((bsz, c, w), idx),
            pl.BlockSpec((2, w), lambda j: (0, 0)),
        ]
        args += [fin, y_other, ln]
        out_shape_y = jax.ShapeDtypeStruct((bsz, rows, n_cols * w), F32)
        y_spec = pl.BlockSpec((bsz, c, w), lambda j: (0, blk(j) % cpc, blk(j) // cpc))
    else:
        out_shape_y = jax.ShapeDtypeStruct((bsz, seqlen, w), F32)
        y_spec = pl.BlockSpec((bsz, c, w), idx)
    y, state = pl.pallas_call(
        functools.partial(_rw_scan_kernel, backward=backward, final=final),
        grid=(n,),
        in_specs=in_specs,
        out_specs=[y_spec, st_spec],
        out_shape=[out_shape_y, jax.ShapeDtypeStruct(state0.shape, F32)],
        scratch_shapes=[pltpu.VMEM(state0.shape, F32)],
        compiler_params=_params("arbitrary"),
        name="rw_scan_bwd" if backward else "rw_scan_fwd",
    )(*args)
    if final:
        y = y.reshape(bsz, seqlen, w)
    return y, state


def _outproj_kernel(h_ref, mod_ref, dn_ref, rw_ref, wdn_ref, wrw_ref, o_ref):
    mix = _dot(dn_ref[0].astype(BF16), wdn_ref[...]) + _dot(rw_ref[0].astype(BF16), wrw_ref[...])
    o_ref[0] = h_ref[0] + mod_ref[0, 5:6, :] * mix


def _outproj(h, mod, dn, rw, w_out, *, tm):
    bsz, seqlen, d = h.shape
    tok = lambda width: pl.BlockSpec((1, tm, width), lambda b, i: (b, i, 0))
    return pl.pallas_call(
        _outproj_kernel,
        grid=(bsz, seqlen // tm),
        in_specs=[
            tok(d),
            pl.BlockSpec((1, N_MOD, d), lambda b, i: (b, 0, 0)),
            tok(DN_WIDTH), tok(RW_WIDTH),
            pl.BlockSpec((DN_WIDTH, d), lambda b, i: (0, 0)),
            pl.BlockSpec((RW_WIDTH, d), lambda b, i: (1, 0)),
        ],
        out_specs=tok(d),
        out_shape=jax.ShapeDtypeStruct(h.shape, F32),
        compiler_params=_params("parallel", "parallel"),
        name="outproj",
    )(h, mod, dn, rw, w_out, w_out)


def _final_norm_kernel(h_ref, g_ref, o_ref):
    h = h_ref[0]
    o_ref[0] = h * lax.rsqrt(jnp.mean(h * h, axis=-1, keepdims=True) + NORM_EPS) * g_ref[...]


def _final_norm(h, gain, *, tm):
    bsz, seqlen, d = h.shape
    return pl.pallas_call(
        _final_norm_kernel,
        grid=(bsz, seqlen // tm),
        in_specs=[pl.BlockSpec((1, tm, d), lambda b, i: (b, i, 0)), pl.BlockSpec((1, d), lambda b, i: (0, 0))],
        out_specs=pl.BlockSpec((1, tm, d), lambda b, i: (b, i, 0)),
        out_shape=jax.ShapeDtypeStruct(h.shape, F32),
        compiler_params=_params("parallel", "parallel"),
        name="final_norm",
    )(h, gain.reshape(1, d))


def _pad_rows(x, rows):
    return jnp.pad(x, ((0, rows - x.shape[0]), (0, 0)))


def _block_diag2(m):
    z = jnp.zeros_like(m[0])
    return jnp.concatenate([jnp.concatenate([m[0], z], axis=1), jnp.concatenate([z, m[1]], axis=1)], axis=0)


def _token_mixing(h_ctx, h_lat, mod_ctx, mod_lat, gain, lw, *, need_ctx_out, tm_lat, tm_ctx):
    bsz = h_lat.shape[0]
    n_cols = GRID_W
    outs = {}
    dn_state = jnp.zeros((2, bsz, DN_HEADS, DN_HEAD_DIM, DN_HEAD_DIM), F32)
    rw_state = jnp.zeros((2, bsz, RW_PAIRS, RW_PAIR, RW_PAIR), F32)
    dn_states = [dn_state[0], dn_state[1]]
    rw_states = [rw_state[0], rw_state[1]]
    for name, h, mod, tm, cols in (("ctx", h_ctx, mod_ctx, tm_ctx, 1), ("lat", h_lat, mod_lat, tm_lat, n_cols)):
        p_dn, p_rw = _proj(h, mod, gain, lw["w_dn"], lw["w_rw"], tm=min(tm, 512))
        qkv, gb = _dn_features(p_dn, lw["conv_w"], lw["alog_row"], lw["dt_row"], rows=min(512, h.shape[1]))
        o_b, dn_states[1] = _dn_scan(qkv, gb, dn_states[1], backward=True)
        dn_out, dn_states[0] = _dn_scan(qkv, gb, dn_states[0], backward=False, p_dn=p_dn, o_other=o_b,
                                        norm_w=lw["dn_norm"])
        shared, dir0, dir1, fin = _rw_features(p_rw, cols, lw["rw_feat"])
        y_b, rw_states[1] = _rw_scan(shared, dir1, rw_states[1], cols, backward=True)
        rw_out, rw_states[0] = _rw_scan(shared, dir0, rw_states[0], cols, backward=False, fin=fin, y_other=y_b,
                                        ln=lw["rw_ln"])
        if name == "lat" or need_ctx_out:
            outs[name] = _outproj(h, mod, dn_out, rw_out, lw["w_out"], tm=tm)
    return outs.get("ctx"), outs["lat"]


def kernel(x, c, ctx, c_ctx, ada_w, ada_b, norm_g, ffn_w13, ffn_w2, w_in, dn_conv, dn_a_log, dn_dt_bias, dn_norm,
           rw_mu, rw_w0, rw_w2, rw_a0, rw_a2, rw_g2, rw_kk, rw_ka, rw_u, rw_ln, w_out, final_norm):
    depth = ada_w.shape[0]
    bsz, seqlen, d = x.shape
    ctx_len = ctx.shape[1]
    tm_lat, tm_ctx = 1024, ctx_len

    s_rows = jnp.concatenate([c_ctx[None, :], c, jnp.zeros((8 - 1 - bsz, d), F32)], axis=0)
    mods = _adaln(s_rows, ada_w, ada_b).reshape(depth, 8, N_MOD, d)

    w13_bf, w2_bf = ffn_w13.astype(BF16), ffn_w2.astype(BF16)
    dn_cols = 4 * DN_WIDTH + 4 * DN_HEADS
    w_dn = jnp.pad(w_in[:, :, :dn_cols], ((0, 0), (0, 0), (0, DN_P_WIDTH - dn_cols))).astype(BF16)
    w_rw = w_in[:, :, dn_cols:].astype(BF16)
    w_out_bf = w_out.astype(BF16)
    pad_lanes = lambda v: jnp.pad(v.reshape(1, -1), ((0, 0), (0, DN_AB_PAD - v.size)))

    h_lat, h_ctx = x, ctx
    for i in range(depth):
        last = i == depth - 1
        mod_ctx = jnp.broadcast_to(mods[i, 0:1], (bsz, N_MOD, d))
        mod_lat = mods[i, 1:1 + bsz]
        lw = {
            "w_dn": w_dn[i], "w_rw": w_rw[i], "w_out": w_out_bf[i],
            "conv_w": _pad_rows(dn_conv[i], 8),
            "alog_row": pad_lanes(dn_a_log[i]), "dt_row": pad_lanes(dn_dt_bias[i]),
            "dn_norm": dn_norm[i],
            "rw_feat": (
                _pad_rows(rw_mu[i], 8),
                rw_w0[i].reshape(1, -1), _block_diag2(rw_w2[i]).astype(BF16),
                rw_a0[i].reshape(1, -1), _block_diag2(rw_a2[i]).astype(BF16),
                rw_g2[i].astype(BF16),
                rw_kk[i].reshape(1, -1), rw_ka[i].reshape(1, -1), rw_u[i].reshape(1, -1),
            ),
            "rw_ln": rw_ln[i],
        }
        h_lat = _ffn(h_lat, mod_lat, norm_g[i, 0], w13_bf[i, 0], w2_bf[i, 0], row0=0, tm=tm_lat)
        h_ctx = _ffn(h_ctx, mod_ctx, norm_g[i, 0], w13_bf[i, 0], w2_bf[i, 0], row0=0, tm=tm_ctx)
        h_ctx_mix, h_lat = _token_mixing(h_ctx, h_lat, mod_ctx, mod_lat, norm_g[i, 1], lw,
                                         need_ctx_out=not last, tm_lat=tm_lat, tm_ctx=tm_ctx)
        h_lat = _ffn(h_lat, mod_lat, norm_g[i, 2], w13_bf[i, 1], w2_bf[i, 1], row0=6, tm=tm_lat)
        if not last:
            h_ctx = _ffn(h_ctx_mix, mod_ctx, norm_g[i, 2], w13_bf[i, 1], w2_bf[i, 1], row0=6, tm=tm_ctx)
    return _final_norm(h_lat, final_norm, tm=tm_lat)
```

```python
import functools

import jax
import jax.numpy as jnp
from jax import lax
from jax.experimental import pallas as pl
from jax.experimental.pallas import tpu as pltpu

F32 = jnp.float32
BF16 = jnp.bfloat16
HIGHEST = lax.Precision.HIGHEST

NORM_EPS = 1e-6
RW_GN_EPS = 64e-5
N_MOD = 9
GRID_W = 64
SHORT_CONV = 5

DN_HEADS = 4
DN_HEAD_DIM = 128
DN_WIDTH = DN_HEADS * DN_HEAD_DIM
DN_AB_PAD = 128
DN_P_WIDTH = 4 * DN_WIDTH + DN_AB_PAD
DN_CHUNK = 128

RW_HEADS = 8
RW_HEAD_DIM = 64
RW_WIDTH = RW_HEADS * RW_HEAD_DIM
RW_LORA = 128
RW_P_WIDTH = 3 * RW_WIDTH + 3 * RW_LORA
RW_CHUNK = 64
RW_PAIR = 2 * RW_HEAD_DIM
RW_PAIRS = RW_WIDTH // RW_PAIR

HALO = 8
V7X_VMEM_LIMIT = 56 * 1024 * 1024


def _params(*sem):
    return pltpu.CompilerParams(dimension_semantics=sem, vmem_limit_bytes=V7X_VMEM_LIMIT)


def _dot(a, b, precision=None):
    return jnp.dot(a, b, preferred_element_type=F32, precision=precision)


def _dot_nt(a, b):
    return lax.dot_general(a, b, (((1,), (1,)), ((), ())), preferred_element_type=F32)


def _dot_tn(a, b):
    return lax.dot_general(a, b, (((0,), (0,)), ((), ())), preferred_element_type=F32)


def _bdot(a, b):
    return _dot(a.astype(BF16), b.astype(BF16))


def _dot_split3(a01, x):
    x1, x2, x3 = _split3(x)
    return _dot(a01, x1) + _dot(a01, x2) + _dot(a01, x3)


def _split3(x):
    x1 = x.astype(BF16)
    r1 = x - x1.astype(F32)
    x2 = r1.astype(BF16)
    x3 = (r1 - x2.astype(F32)).astype(BF16)
    return x1, x2, x3


def _silu(x):
    return x * jax.nn.sigmoid(x)


def _softplus(x):
    return jnp.maximum(x, 0.0) + jnp.log1p(jnp.exp(-jnp.abs(x)))


def _rms_modulate(h, gain, shift, scale):
    y = h * lax.rsqrt(jnp.mean(h * h, axis=-1, keepdims=True) + NORM_EPS) * gain
    return y * (1.0 + scale) + shift


def _unit_tri_inverse(a, size):
    n = a.shape[0]
    diff = lax.broadcasted_iota(jnp.int32, (n, n), 0) ^ lax.broadcasted_iota(jnp.int32, (n, n), 1)
    t = jnp.where(diff == 0, 1.0, jnp.where(diff == 1, -a, 0.0))
    level = 1
    while (1 << level) < size:
        l = jnp.where((diff >> level) == 1, a, 0.0)
        tl = _bdot(t, l)
        yield
        t = t - _bdot(tl, t)
        yield
        level += 1
    return t


def _run_lockstep(gens):
    results = [None] * len(gens)
    live = list(enumerate(gens))
    while live:
        still = []
        for i, g in live:
            try:
                next(g)
                still.append((i, g))
            except StopIteration as done:
                results[i] = done.value
        live = still
    return results


def _shift_rows(x, halo_prev, halo_next, s):
    rows = x.shape[0]
    row8 = lax.broadcasted_iota(jnp.int32, (HALO, x.shape[1]), 0)
    back = pltpu.roll(x, s, 0)
    hp = pltpu.roll(halo_prev, s, 0)
    back = jnp.concatenate([jnp.where(row8 < s, hp, back[:HALO]), back[HALO:]], axis=0)
    fwd = pltpu.roll(x, rows - s, 0)
    hn = pltpu.roll(halo_next, HALO - s, 0)
    fwd = jnp.concatenate([fwd[:rows - HALO], jnp.where(row8 >= HALO - s, hn, fwd[rows - HALO:])], axis=0)
    return back, fwd


def _adaln_kernel(s_ref, w_ref, b_ref, o_ref):
    s = _silu(s_ref[...])
    o_ref[0] = _dot(s, w_ref[0], HIGHEST) + b_ref[0]


def _adaln(s_rows, ada_w, ada_b):
    depth, d, n = ada_w.shape
    tn = n // 8
    return pl.pallas_call(
        _adaln_kernel,
        grid=(depth, n // tn),
        in_specs=[
            pl.BlockSpec((8, d), lambda l, j: (0, 0)),
            pl.BlockSpec((1, d, tn), lambda l, j: (l, 0, j)),
            pl.BlockSpec((1, 1, tn), lambda l, j: (l, 0, j)),
        ],
        out_specs=pl.BlockSpec((1, 8, tn), lambda l, j: (l, 0, j)),
        out_shape=jax.ShapeDtypeStruct((depth, 8, n), F32),
        compiler_params=_params("parallel", "parallel"),
        name="adaln",
    )(s_rows, ada_w, ada_b.reshape(depth, 1, n))


def _ffn_kernel(h_ref, mod_ref, g_ref, w1_ref, w3_ref, w2_ref, *rest, row0, final_norm):
    if final_norm:
        fg_ref, o_ref, xn_ref, acc_ref = rest
    else:
        o_ref, xn_ref, acc_ref = rest
    k = pl.program_id(2)

    @pl.when(k == 0)
    def _():
        xn = _rms_modulate(h_ref[0], g_ref[...], mod_ref[0, row0:row0 + 1, :], mod_ref[0, row0 + 1:row0 + 2, :])
        xn_ref[...] = xn.astype(BF16)
        acc_ref[...] = jnp.zeros_like(acc_ref)

    xn = xn_ref[...]
    gate = _dot(xn, w1_ref[...])
    up = _dot(xn, w3_ref[...])
    acc_ref[...] += _dot((_silu(gate) * up).astype(BF16), w2_ref[...])

    @pl.when(k == pl.num_programs(2) - 1)
    def _():
        out = h_ref[0] + 0.5 * mod_ref[0, row0 + 2:row0 + 3, :] * acc_ref[...]
        if final_norm:
            out = out * lax.rsqrt(jnp.mean(out * out, axis=-1, keepdims=True) + NORM_EPS) * fg_ref[...]
        o_ref[0] = out


def _ffn(h, mod, gain, w13, w2, *, row0, tm, final_gain=None):
    bsz, seqlen, d = h.shape
    dff = w2.shape[0]
    tf = dff // 2
    nk = dff // tf
    row = pl.BlockSpec((1, d), lambda b, i, k: (0, 0))
    in_specs = [
        pl.BlockSpec((1, tm, d), lambda b, i, k: (b, i, 0)),
        pl.BlockSpec((1, N_MOD, d), lambda b, i, k: (b, 0, 0)),
        row,
        pl.BlockSpec((d, tf), lambda b, i, k: (0, k)),
        pl.BlockSpec((d, tf), lambda b, i, k: (0, k + nk)),
        pl.BlockSpec((tf, d), lambda b, i, k: (k, 0)),
    ]
    args = [h, mod, gain.reshape(1, d), w13, w13, w2]
    if final_gain is not None:
        in_specs.append(row)
        args.append(final_gain.reshape(1, d))
    return pl.pallas_call(
        functools.partial(_ffn_kernel, row0=row0, final_norm=final_gain is not None),
        grid=(bsz, seqlen // tm, nk),
        in_specs=in_specs,
        out_specs=pl.BlockSpec((1, tm, d), lambda b, i, k: (b, i, 0)),
        out_shape=jax.ShapeDtypeStruct(h.shape, F32),
        scratch_shapes=[pltpu.VMEM((tm, d), BF16), pltpu.VMEM((tm, d), F32)],
        compiler_params=_params("parallel", "parallel", "arbitrary"),
        name="ffn",
    )(*args)


def _proj_kernel(h_ref, mod_ref, g_ref, wdn_ref, wrw_ref, pdn_ref, prw_ref):
    xn = _rms_modulate(h_ref[0], g_ref[...], mod_ref[0, 3:4, :], mod_ref[0, 4:5, :]).astype(BF16)
    pdn_ref[0] = _dot(xn, wdn_ref[...])
    prw_ref[0] = _dot(xn, wrw_ref[...])


def _proj(h, mod, gain, w_dn, w_rw, *, tm):
    bsz, seqlen, d = h.shape
    return pl.pallas_call(
        _proj_kernel,
        grid=(bsz, seqlen // tm),
        in_specs=[
            pl.BlockSpec((1, tm, d), lambda b, i: (b, i, 0)),
            pl.BlockSpec((1, N_MOD, d), lambda b, i: (b, 0, 0)),
            pl.BlockSpec((1, d), lambda b, i: (0, 0)),
            pl.BlockSpec((d, DN_P_WIDTH), lambda b, i: (0, 0)),
            pl.BlockSpec((d, RW_P_WIDTH), lambda b, i: (0, 0)),
        ],
        out_specs=[
            pl.BlockSpec((1, tm, DN_P_WIDTH), lambda b, i: (b, i, 0)),
            pl.BlockSpec((1, tm, RW_P_WIDTH), lambda b, i: (b, i, 0)),
        ],
        out_shape=[
            jax.ShapeDtypeStruct((bsz, seqlen, DN_P_WIDTH), F32),
            jax.ShapeDtypeStruct((bsz, seqlen, RW_P_WIDTH), F32),
        ],
        compiler_params=_params("parallel", "parallel"),
        name="proj",
    )(h, mod, gain.reshape(1, d), w_dn, w_rw)


def _dn_feat_kernel(x_ref, xp_ref, xn_ref, ab_ref, cw_ref, alog_ref, dt_ref, qkv_ref, gb_ref):
    i = pl.program_id(1)
    x = x_ref[0]
    hp = jnp.where(i == 0, 0.0, xp_ref[0])
    hn = jnp.where(i == pl.num_programs(1) - 1, 0.0, xn_ref[0])
    b1, f1 = _shift_rows(x, hp, hn, 1)
    b2, f2 = _shift_rows(x, hp, hn, 2)
    conv = cw_ref[0:1] * b2 + cw_ref[1:2] * b1 + cw_ref[2:3] * x + cw_ref[3:4] * f1 + cw_ref[4:5] * f2
    act = _silu(conv)
    for hd in range(2 * DN_HEADS):
        t = act[:, hd * DN_HEAD_DIM:(hd + 1) * DN_HEAD_DIM]
        t = t * lax.rsqrt(jnp.sum(t * t, axis=-1, keepdims=True) + NORM_EPS)
        if hd < DN_HEADS:
            t = t * DN_HEAD_DIM ** -0.5
        qkv_ref[0, :, hd * DN_HEAD_DIM:(hd + 1) * DN_HEAD_DIM] = t.astype(BF16)
    qkv_ref[0, :, 2 * DN_WIDTH:] = act[:, 2 * DN_WIDTH:].astype(BF16)
    ab = ab_ref[0]
    lane = lax.broadcasted_iota(jnp.int32, ab.shape, 1)
    g = -jnp.exp(alog_ref[...]) * _softplus(ab + dt_ref[...])
    rows = x.shape[0]
    ri = lax.broadcasted_iota(jnp.int32, (rows, rows), 0)
    ci = lax.broadcasted_iota(jnp.int32, (rows, rows), 1)
    same = (ri // DN_CHUNK) == (ci // DN_CHUNK)
    gc_f = _dot_split3(jnp.where(same & (ri >= ci), 1.0, 0.0).astype(BF16), g)
    gc_b = _dot_split3(jnp.where(same & (ri <= ci), 1.0, 0.0).astype(BF16), g)
    gb_ref[0] = jnp.where(lane < DN_HEADS, gc_f, jnp.where(lane < 2 * DN_HEADS, gc_b, jax.nn.sigmoid(ab)))


def _dn_features(p_dn, conv_w, alog_row, dt_row, *, rows):
    bsz, seqlen, _ = p_dn.shape
    nblk = seqlen // rows
    hpb = rows // HALO
    return pl.pallas_call(
        _dn_feat_kernel,
        grid=(bsz, nblk),
        in_specs=[
            pl.BlockSpec((1, rows, 3 * DN_WIDTH), lambda b, i: (b, i, 0)),
            pl.BlockSpec((1, HALO, 3 * DN_WIDTH), lambda b, i: (b, jnp.maximum(i * hpb - 1, 0), 0)),
            pl.BlockSpec((1, HALO, 3 * DN_WIDTH), lambda b, i: (b, jnp.minimum((i + 1) * hpb, nblk * hpb - 1), 0)),
            pl.BlockSpec((1, rows, DN_AB_PAD), lambda b, i: (b, i, 4 * DN_WIDTH // DN_AB_PAD)),
            pl.BlockSpec((8, 3 * DN_WIDTH), lambda b, i: (0, 0)),
            pl.BlockSpec((1, DN_AB_PAD), lambda b, i: (0, 0)),
            pl.BlockSpec((1, DN_AB_PAD), lambda b, i: (0, 0)),
        ],
        out_specs=[
            pl.BlockSpec((1, rows, 3 * DN_WIDTH), lambda b, i: (b, i, 0)),
            pl.BlockSpec((1, rows, DN_AB_PAD), lambda b, i: (b, i, 0)),
        ],
        out_shape=[
            jax.ShapeDtypeStruct((bsz, seqlen, 3 * DN_WIDTH), BF16),
            jax.ShapeDtypeStruct((bsz, seqlen, DN_AB_PAD), F32),
        ],
        compiler_params=_params("parallel", "parallel"),
        name="dn_feat",
    )(p_dn, p_dn, p_dn, p_dn, conv_w, alog_row, dt_row)


def _dn_scan_kernel(*refs, backward, final):
    if final:
        qkv_ref, gb_ref, s0_ref, z_ref, ob_ref, nw_ref, o_ref, sn_ref, st_ref = refs
    else:
        qkv_ref, gb_ref, s0_ref, o_ref, sn_ref, st_ref = refs
    j = pl.program_id(0)

    @pl.when(j == 0)
    def _():
        st_ref[...] = s0_ref[...]

    bsz, c = qkv_ref.shape[0], qkv_ref.shape[1]
    ri = lax.broadcasted_iota(jnp.int32, (c, c), 0)
    ci = lax.broadcasted_iota(jnp.int32, (c, c), 1)
    incl = (ri <= ci) if backward else (ri >= ci)
    strict = (ri < ci) if backward else (ri > ci)
    d = 1 if backward else 0
    last = 0 if backward else c - 1

    def chunk(q, k, v, beta, gcc, gcr, gcl, s):
        decay = jnp.where(incl, jnp.exp(jnp.where(incl, gcc - gcr, 0.0)), 0.0)
        kb = k.astype(BF16)
        kq = _dot_nt(jnp.concatenate([kb, q.astype(BF16)], axis=0), kb)
        yield
        a = jnp.where(strict, kq[:c] * decay * beta, 0.0)
        attn = jnp.where(incl, kq[c:] * decay, 0.0)
        t_inv = yield from _unit_tri_inverse(a, c)
        e_gc = jnp.exp(gcc)
        wu = _bdot(t_inv, jnp.concatenate([k * (beta * e_gc), v * beta], axis=1))
        yield
        wq = _bdot(jnp.concatenate([wu[:, :DN_HEAD_DIM], q * e_gc], axis=0), s)
        yield
        v_new = wu[:, DN_HEAD_DIM:] - wq[:c]
        o = wq[c:] + _bdot(attn, v_new)
        k_dec = k * jnp.exp(gcl - gcc)
        s_new = s * jnp.exp(gcl) + _dot_tn(k_dec.astype(BF16), v_new.astype(BF16))
        return o, s_new

    heads = [(b, hd) for b in range(bsz) for hd in range(DN_HEADS)]
    head_cols = lambda base, hd: slice(base + hd * DN_HEAD_DIM, base + (hd + 1) * DN_HEAD_DIM)
    gbs = [gb_ref[b] for b in range(bsz)]
    gc_cols = gbs
    gc_rows = [gb.T for gb in gbs]
    gens = []
    for b, hd in heads:
        col = DN_HEADS * d + hd
        gens.append(chunk(
            qkv_ref[b, :, head_cols(0, hd)], qkv_ref[b, :, head_cols(DN_WIDTH, hd)],
            qkv_ref[b, :, head_cols(2 * DN_WIDTH, hd)],
            gbs[b][:, 2 * DN_HEADS + col:2 * DN_HEADS + col + 1],
            gc_cols[b][:, col:col + 1], gc_rows[b][col:col + 1, :], gc_cols[b][last:last + 1, col:col + 1],
            st_ref[b, hd]))
    results = _run_lockstep(gens)
    for (b, hd), (o, s_new) in zip(heads, results):
        st_ref[b, hd] = s_new
        if final:
            o = o + ob_ref[b, :, head_cols(0, hd)]
            o = o * lax.rsqrt(jnp.mean(o * o, axis=-1, keepdims=True) + NORM_EPS) * nw_ref[...]
            o = o * _silu(z_ref[b, :, head_cols(0, hd)])
        o_ref[b, :, head_cols(0, hd)] = o

    @pl.when(j == pl.num_programs(0) - 1)
    def _():
        sn_ref[...] = st_ref[...]


def _dn_scan(qkv, gb, state0, *, backward, p_dn=None, o_other=None, norm_w=None):
    bsz, seqlen, _ = qkv.shape
    c = DN_CHUNK
    n = seqlen // c
    final = p_dn is not None
    idx = (lambda j: (0, n - 1 - j, 0)) if backward else (lambda j: (0, j, 0))
    st_spec = pl.BlockSpec((bsz, DN_HEADS, DN_HEAD_DIM, DN_HEAD_DIM), lambda j: (0, 0, 0, 0))
    in_specs = [
        pl.BlockSpec((bsz, c, 3 * DN_WIDTH), idx),
        pl.BlockSpec((bsz, c, DN_AB_PAD), idx),
        st_spec,
    ]
    args = [qkv, gb, state0]
    if final:
        zcol = 3 * DN_WIDTH // DN_WIDTH
        in_specs += [
            pl.BlockSpec((bsz, c, DN_WIDTH), lambda j: (0, idx(j)[1], zcol)),
            pl.BlockSpec((bsz, c, DN_WIDTH), idx),
            pl.BlockSpec((1, DN_HEAD_DIM), lambda j: (0, 0)),
        ]
        args += [p_dn, o_other, norm_w.reshape(1, DN_HEAD_DIM)]
    return pl.pallas_call(
        functools.partial(_dn_scan_kernel, backward=backward, final=final),
        grid=(n,),
        in_specs=in_specs,
        out_specs=[pl.BlockSpec((bsz, c, DN_WIDTH), idx), st_spec],
        out_shape=[
            jax.ShapeDtypeStruct((bsz, seqlen, DN_WIDTH), F32),
            jax.ShapeDtypeStruct(state0.shape, F32),
        ],
        scratch_shapes=[pltpu.VMEM(state0.shape, F32)],
        compiler_params=_params("arbitrary"),
        name="dn_scan_bwd" if backward else "dn_scan_fwd",
    )(*args)


def _head_sum(x, ones_bd):
    pieces = _split3(x)
    parts = []
    for p in range(RW_PAIRS):
        sl = slice(p * RW_PAIR, (p + 1) * RW_PAIR)
        parts.append(_dot(pieces[0][:, sl], ones_bd) + _dot(pieces[1][:, sl], ones_bd) + _dot(pieces[2][:, sl], ones_bd))
    return jnp.concatenate(parts, axis=1)


def _pair_ones():
    ri = lax.broadcasted_iota(jnp.int32, (RW_PAIR, RW_PAIR), 0)
    ci = lax.broadcasted_iota(jnp.int32, (RW_PAIR, RW_PAIR), 1)
    return ((ri // RW_HEAD_DIM) == (ci // RW_HEAD_DIM)).astype(F32)


def _rw_feat_kernel(x_ref, xp_ref, xn_ref, mu_ref, w0_ref, w2_ref, a0_ref, a2_ref, g2_ref, kk_ref, ka_ref, u_ref,
                    v_ref, d0_ref, d1_ref, wl0_ref, wl1_ref, fin_ref):
    c = pl.program_id(1)
    x = x_ref[0]
    hp = jnp.where(c == 0, 0.0, xp_ref[0])
    hn = jnp.where(c == pl.num_programs(1) - 1, 0.0, xn_ref[0])
    prev, nxt = _shift_rows(x, hp, hn, 1)
    p = x + mu_ref[0:1] * (prev - x) + mu_ref[1:2] * (nxt - x)
    w = RW_WIDTH
    r, k, v = p[:, :w], p[:, w:2 * w], p[:, 2 * w:3 * w]
    lw = p[:, 3 * w:3 * w + RW_LORA]
    la = p[:, 3 * w + RW_LORA:3 * w + 2 * RW_LORA]
    lg = p[:, 3 * w + 2 * RW_LORA:]
    w_log = -_softplus(-(w0_ref[...] + _bdot(jnp.tanh(lw), w2_ref[...]))) - 0.5
    log_decay = -jnp.exp(w_log)
    a = jax.nn.sigmoid(a0_ref[...] + _bdot(la, a2_ref[...]))
    gate = _bdot(jax.nn.sigmoid(lg), g2_ref[...])
    ones_bd = _pair_ones().astype(BF16)
    kx = k * kk_ref[...]
    kk = kx * lax.rsqrt(_head_sum(kx * kx, ones_bd) + NORM_EPS)
    v_ref[0] = v.astype(BF16)
    rows = x.shape[0]
    n_chunks = rows // RW_CHUNK
    ri = lax.broadcasted_iota(jnp.int32, (rows, rows), 0)
    ci = lax.broadcasted_iota(jnp.int32, (rows, rows), 1)
    same = (ri // RW_CHUNK) == (ci // RW_CHUNK)
    bonus = jnp.zeros_like(v)
    for d, (d_ref, wl_ref) in enumerate(((d0_ref, wl0_ref), (d1_ref, wl1_ref))):
        a_d = a[:, d * w:(d + 1) * w]
        k_dir = k * (1.0 + (a_d - 1.0) * ka_ref[...])
        b_dir = a_d * kk
        bonus = bonus + _head_sum(r * k_dir * u_ref[:, d * w:(d + 1) * w], ones_bd) * v
        ld = log_decay[:, d * w:(d + 1) * w]
        tri = jnp.where(same & ((ri >= ci) if d == 0 else (ri <= ci)), 1.0, 0.0).astype(BF16)
        cum = _dot_split3(tri, ld)
        last_row = lambda q: q * RW_CHUNK + (RW_CHUNK - 1 if d == 0 else 0)
        last = [cum[last_row(q):last_row(q) + 1, :] for q in range(n_chunks)]
        cum_last = jnp.concatenate([jnp.broadcast_to(t, (RW_CHUNK, w)) for t in last], axis=0)
        e_in, e_out, e_tail = jnp.exp(cum), jnp.exp(-cum), jnp.exp(cum_last - cum)
        d_ref[0, :, 0 * w:1 * w] = (r * e_in).astype(BF16)
        d_ref[0, :, 1 * w:2 * w] = (kk * jnp.exp(cum - ld)).astype(BF16)
        d_ref[0, :, 2 * w:3 * w] = (k_dir * e_out).astype(BF16)
        d_ref[0, :, 3 * w:4 * w] = (b_dir * e_out).astype(BF16)
        d_ref[0, :, 4 * w:5 * w] = (k_dir * e_tail).astype(BF16)
        d_ref[0, :, 5 * w:6 * w] = (b_dir * e_tail).astype(BF16)
        wl_ref[0] = jnp.concatenate([jnp.broadcast_to(jnp.exp(t), (8, w)) for t in last], axis=0)
    fin_ref[0, :, :w] = gate
    fin_ref[0, :, w:] = bonus


def _rw_features(p_rw, n_cols, wts):
    bsz, seqlen, _ = p_rw.shape
    rows = seqlen // n_cols
    pv = p_rw.reshape(bsz, rows, n_cols * RW_P_WIDTH)
    hpb = rows // HALO
    const = lambda shape: pl.BlockSpec(shape, lambda b, c: (0, 0))
    w = RW_WIDTH
    out_spec = lambda width: pl.BlockSpec((1, rows, width), lambda b, c: (b, c, 0))
    wl_spec = pl.BlockSpec((1, rows // RW_CHUNK * 8, w), lambda b, c: (b, c, 0))
    return pl.pallas_call(
        _rw_feat_kernel,
        grid=(bsz, n_cols),
        in_specs=[
            pl.BlockSpec((1, rows, RW_P_WIDTH), lambda b, c: (b, 0, c)),
            pl.BlockSpec((1, HALO, RW_P_WIDTH), lambda b, c: (b, hpb - 1, jnp.maximum(c - 1, 0))),
            pl.BlockSpec((1, HALO, RW_P_WIDTH), lambda b, c: (b, 0, jnp.minimum(c + 1, n_cols - 1))),
            const((8, RW_P_WIDTH)),
            const((1, 2 * w)), const((RW_LORA, 2 * w)),
            const((1, 2 * w)), const((RW_LORA, 2 * w)),
            const((RW_LORA, w)),
            const((1, w)), const((1, w)), const((1, 2 * w)),
        ],
        out_specs=[out_spec(w), out_spec(6 * w), out_spec(6 * w), wl_spec, wl_spec, out_spec(2 * w)],
        out_shape=[
            jax.ShapeDtypeStruct((bsz, seqlen, w), BF16),
            jax.ShapeDtypeStruct((bsz, seqlen, 6 * w), BF16),
            jax.ShapeDtypeStruct((bsz, seqlen, 6 * w), BF16),
            jax.ShapeDtypeStruct((bsz, seqlen // RW_CHUNK * 8, w), F32),
            jax.ShapeDtypeStruct((bsz, seqlen // RW_CHUNK * 8, w), F32),
            jax.ShapeDtypeStruct((bsz, seqlen, 2 * w), F32),
        ],
        compiler_params=_params("parallel", "parallel"),
        name="rw_feat",
    )(pv, pv, pv, *wts)


def _stack_heads(x, lane_lo):
    return jnp.concatenate([jnp.where(lane_lo, x, 0.0), jnp.where(lane_lo, 0.0, x)], axis=0)


def _rw_scan_kernel(*refs, backward, final):
    if final:
        v_ref, dr_ref, wl_ref, s0_ref, fin_ref, yb_ref, ln_ref, y_ref, sn_ref, st_ref = refs
    else:
        v_ref, dr_ref, wl_ref, s0_ref, y_ref, sn_ref, st_ref = refs
    j = pl.program_id(0)

    @pl.when(j == 0)
    def _():
        st_ref[...] = s0_ref[...]

    bsz, c = v_ref.shape[0], v_ref.shape[1]
    w = RW_WIDTH
    c2 = 2 * c
    ri2 = lax.broadcasted_iota(jnp.int32, (c2, c2), 0)
    ci2 = lax.broadcasted_iota(jnp.int32, (c2, c2), 1)
    incl = (ri2 <= ci2) if backward else (ri2 >= ci2)
    strict = (ri2 < ci2) if backward else (ri2 > ci2)
    pi =lax.broadcasted_iota(jnp.int32, (RW_PAIR, RW_PAIR), 0)
    pj = lax.broadcasted_iota(jnp.int32, (RW_PAIR, RW_PAIR), 1)
    lane_lo = lax.broadcasted_iota(jnp.int32, (c, RW_PAIR), 1) < RW_HEAD_DIM

    def head_mean(t):
        lo = jnp.sum(jnp.where(lane_lo, t, 0.0), axis=-1, keepdims=True)
        hi = jnp.sum(jnp.where(lane_lo, 0.0, t), axis=-1, keepdims=True)
        return jnp.where(lane_lo, lo, hi) * (1.0 / RW_HEAD_DIM)

    def pair_chunk(rts, kts, kis, bis, kds, bds, vs, w_last, h0):
        g = _dot_nt(jnp.concatenate([kts, rts], axis=0), jnp.concatenate([bis, kis], axis=0))
        yield
        a_b = jnp.where(strict, g[:c2, :c2], 0.0)
        a_k = jnp.where(strict, g[:c2, c2:], 0.0)
        a_rb = jnp.where(incl, g[c2:, :c2], 0.0)
        a_rk = jnp.where(incl, g[c2:, c2:], 0.0)
        x = _bdot(a_k, vs)
        t_inv = yield from _unit_tri_inverse(a_b, c)
        uk = _bdot(t_inv, jnp.concatenate([x.astype(BF16), kts], axis=1))
        yield
        zeros = jnp.zeros((c2, RW_PAIR), BF16)
        rhs = jnp.concatenate([jnp.concatenate([vs, zeros], axis=1), uk.astype(BF16)], axis=0)
        yr = _bdot(jnp.concatenate([a_rk, -a_rb], axis=1), rhs)
        nm = _dot_tn(jnp.concatenate([kds, -bds], axis=0), rhs)
        yield
        m_full = nm[:, RW_PAIR:] + jnp.where(pi == pj, w_last, 0.0)
        r_hat = rts.astype(F32) + yr[:, RW_PAIR:]
        yh = _bdot(jnp.concatenate([r_hat, m_full], axis=0), h0)
        ys = yh[:c2] + yr[:, :RW_PAIR]
        return ys[:c] + ys[c:], yh[c2:] + nm[:, :RW_PAIR]

    pairs, gens = [], []
    for b in range(bsz):
        for p in range(RW_PAIRS):
            pairs.append((b, p))
            operands = [_stack_heads(dr_ref[b, :, i * w + p * RW_PAIR:i * w + (p + 1) * RW_PAIR], lane_lo)
                        for i in range(6)]
            vs = _stack_heads(v_ref[b, :, p * RW_PAIR:(p + 1) * RW_PAIR], lane_lo)
            gens.append(pair_chunk(*operands, vs, wl_ref[b, 0:1, p * RW_PAIR:(p + 1) * RW_PAIR], st_ref[b, p]))
    results = _run_lockstep(gens)
    for (b, p), (y, h_new) in zip(pairs, results):
        sl = slice(p * RW_PAIR, (p + 1) * RW_PAIR)
        st_ref[b, p] = h_new
        if final:
            y = y + yb_ref[b, :, sl] + fin_ref[b, :, w + p * RW_PAIR:w + (p + 1) * RW_PAIR]
            yc = y - head_mean(y)
            y = yc * lax.rsqrt(head_mean(yc * yc) + RW_GN_EPS) * ln_ref[0:1, sl] + ln_ref[1:2, sl]
            y = y * fin_ref[b, :, sl]
        y_ref[b, :, sl] = y

    @pl.when(j == pl.num_programs(0) - 1)
    def _():
        sn_ref[...] = st_ref[...]


def _rw_scan(v, per_dir, w_last, state0, n_cols, *, backward, fin=None, y_other=None, ln=None):
    bsz, seqlen, _ = v.shape
    c = RW_CHUNK
    n = seqlen // c
    w = RW_WIDTH
    final = fin is not None
    blk = (lambda j: n - 1 - j) if backward else (lambda j: j)
    idx = lambda j: (0, blk(j), 0)
    st_spec = pl.BlockSpec(state0.shape, lambda j: (0, 0, 0, 0))
    in_specs = [pl.BlockSpec((bsz, c, w), idx), pl.BlockSpec((bsz, c, 6 * w), idx), pl.BlockSpec((bsz, 8, w), idx),
                st_spec]
    args = [v, per_dir, w_last, state0]
    if final:
        rows = seqlen // n_cols
        cpc = rows // c
        in_specs += [
            pl.BlockSpec((bsz, c, 2 * w), idx),
            pl.BlockSpec((bsz, c, w), idx),
            pl.BlockSpec((2, w), lambda j: (0, 0)),
        ]
        args += [fin, y_other, ln]
        out_shape_y = jax.ShapeDtypeStruct((bsz, rows, n_cols * w), F32)
        y_spec = pl.BlockSpec((bsz, c, w), lambda j: (0, blk(j) % cpc, blk(j) // cpc))
    else:
        out_shape_y = jax.ShapeDtypeStruct((bsz, seqlen, w), F32)
        y_spec = pl.BlockSpec((bsz, c, w), idx)
    y, state = pl.pallas_call(
        functools.partial(_rw_scan_kernel, backward=backward, final=final),
        grid=(n,),
        in_specs=in_specs,
        out_specs=[y_spec, st_spec],
        out_shape=[out_shape_y, jax.ShapeDtypeStruct(state0.shape, F32)],
        scratch_shapes=[pltpu.VMEM(state0.shape, F32)],
        compiler_params=_params("arbitrary"),
        name="rw_scan_bwd" if backward else "rw_scan_fwd",
    )(*args)
    if final:
        y = y.reshape(bsz, seqlen, w)
    return y, state


def _outproj_kernel(h_ref, mod_ref, dn_ref, rw_ref, wdn_ref, wrw_ref, o_ref):
    mix = _dot(dn_ref[0].astype(BF16), wdn_ref[...]) + _dot(rw_ref[0].astype(BF16), wrw_ref[...])
    o_ref[0] = h_ref[0] + mod_ref[0, 5:6, :] * mix


def _outproj(h, mod, dn, rw, w_out, *, tm):
    bsz, seqlen, d = h.shape
    tok = lambda width: pl.BlockSpec((1, tm, width), lambda b, i: (b, i, 0))
    return pl.pallas_call(
        _outproj_kernel,
        grid=(bsz, seqlen // tm),
        in_specs=[
            tok(d),
            pl.BlockSpec((1, N_MOD, d), lambda b, i: (b, 0, 0)),
            tok(DN_WIDTH), tok(RW_WIDTH),
            pl.BlockSpec((DN_WIDTH, d), lambda b, i: (0, 0)),
            pl.BlockSpec((RW_WIDTH, d), lambda b, i: (1, 0)),
        ],
        out_specs=tok(d),
        out_shape=jax.ShapeDtypeStruct(h.shape, F32),
        compiler_params=_params("parallel", "parallel"),
        name="outproj",
    )(h, mod, dn, rw, w_out, w_out)


def _pad_rows(x, rows):
    return jnp.pad(x, ((0, rows - x.shape[0]), (0, 0)))


def _block_diag2(m):
    z = jnp.zeros_like(m[0])
    return jnp.concatenate([jnp.concatenate([m[0], z], axis=1), jnp.concatenate([z, m[1]], axis=1)], axis=0)


def _token_mixing(h_ctx, h_lat, mod_ctx, mod_lat, gain, lw, *, need_ctx_out, tm_lat, tm_ctx):
    bsz = h_lat.shape[0]
    n_cols = GRID_W
    outs = {}
    dn_state = jnp.zeros((2, bsz, DN_HEADS, DN_HEAD_DIM, DN_HEAD_DIM), F32)
    rw_state = jnp.zeros((2, bsz, RW_PAIRS, RW_PAIR, RW_PAIR), F32)
    dn_states = [dn_state[0], dn_state[1]]
    rw_states = [rw_state[0], rw_state[1]]
    for name, h, mod, tm, cols in (("ctx", h_ctx, mod_ctx, tm_ctx, 1), ("lat", h_lat, mod_lat, tm_lat, n_cols)):
        p_dn, p_rw = _proj(h, mod, gain, lw["w_dn"], lw["w_rw"], tm=min(tm, 512))
        qkv, gb = _dn_features(p_dn, lw["conv_w"], lw["alog_row"], lw["dt_row"], rows=min(512, h.shape[1]))
        o_b, dn_states[1] = _dn_scan(qkv, gb, dn_states[1], backward=True)
        dn_out, dn_states[0] = _dn_scan(qkv, gb, dn_states[0], backward=False, p_dn=p_dn, o_other=o_b,
                                        norm_w=lw["dn_norm"])
        v, dir0, dir1, wl0, wl1, fin = _rw_features(p_rw, cols, lw["rw_feat"])
        y_b, rw_states[1] = _rw_scan(v, dir1, wl1, rw_states[1], cols, backward=True)
        rw_out, rw_states[0] = _rw_scan(v, dir0, wl0, rw_states[0], cols, backward=False, fin=fin, y_other=y_b,
                                        ln=lw["rw_ln"])
        if name == "lat" or need_ctx_out:
            outs[name] = _outproj(h, mod, dn_out, rw_out, lw["w_out"], tm=tm)
    return outs.get("ctx"), outs["lat"]


def kernel(x, c, ctx, c_ctx, ada_w, ada_b, norm_g, ffn_w13, ffn_w2, w_in, dn_conv, dn_a_log, dn_dt_bias, dn_norm,
           rw_mu, rw_w0, rw_w2, rw_a0, rw_a2, rw_g2, rw_kk, rw_ka, rw_u, rw_ln, w_out, final_norm):
    depth = ada_w.shape[0]
    bsz, seqlen, d = x.shape
    ctx_len = ctx.shape[1]
    tm_lat, tm_ctx = 1024, ctx_len

    s_rows = jnp.concatenate([c_ctx[None, :], c, jnp.zeros((8 - 1 - bsz, d), F32)], axis=0)
    mods = _adaln(s_rows, ada_w, ada_b).reshape(depth, 8, N_MOD, d)

    w13_bf, w2_bf = ffn_w13.astype(BF16), ffn_w2.astype(BF16)
    dn_cols = 4 * DN_WIDTH + 4 * DN_HEADS
    w_dn = jnp.pad(w_in[:, :, :dn_cols], ((0, 0), (0, 0), (0, DN_P_WIDTH - dn_cols))).astype(BF16)
    w_rw = w_in[:, :, dn_cols:].astype(BF16)
    w_out_bf = w_out.astype(BF16)
    pad_lanes = lambda v: jnp.pad(v.reshape(1, -1), ((0, 0), (0, DN_AB_PAD - v.size)))

    h_lat, h_ctx = x, ctx
    for i in range(depth):
        last = i == depth - 1
        mod_ctx = jnp.broadcast_to(mods[i, 0:1], (bsz, N_MOD, d))
        mod_lat = mods[i, 1:1 + bsz]
        lw = {
            "w_dn": w_dn[i], "w_rw": w_rw[i], "w_out": w_out_bf[i],
            "conv_w": _pad_rows(dn_conv[i], 8),
            "alog_row": pad_lanes(dn_a_log[i]), "dt_row": pad_lanes(dn_dt_bias[i]),
            "dn_norm": dn_norm[i],
            "rw_feat": (
                _pad_rows(rw_mu[i], 8),
                rw_w0[i].reshape(1, -1), _block_diag2(rw_w2[i]).astype(BF16),
                rw_a0[i].reshape(1, -1), _block_diag2(rw_a2[i]).astype(BF16),
                rw_g2[i].astype(BF16),
                rw_kk[i].reshape(1, -1), rw_ka[i].reshape(1, -1), rw_u[i].reshape(1, -1),
            ),
            "rw_ln": rw_ln[i],
        }
        h_lat = _ffn(h_lat, mod_lat, norm_g[i, 0], w13_bf[i, 0], w2_bf[i, 0], row0=0, tm=tm_lat)
        h_ctx = _ffn(h_ctx, mod_ctx, norm_g[i, 0], w13_bf[i, 0], w2_bf[i, 0], row0=0, tm=tm_ctx)
        h_ctx_mix, h_lat = _token_mixing(h_ctx, h_lat, mod_ctx, mod_lat, norm_g[i, 1], lw,
                                         need_ctx_out=not last, tm_lat=tm_lat, tm_ctx=tm_ctx)
        h_lat = _ffn(h_lat, mod_lat, norm_g[i, 2], w13_bf[i, 1], w2_bf[i, 1], row0=6, tm=tm_lat,
                     final_gain=final_norm if last else None)
        if not last:
            h_ctx = _ffn(h_ctx_mix, mod_ctx, norm_g[i, 2], w13_bf[i, 1], w2_bf[i, 1], row0=6, tm=tm_ctx)
    return h_lat
```

```python
import functools

import jax
import jax.numpy as jnp
from jax import lax
from jax.experimental import pallas as pl
from jax.experimental.pallas import tpu as pltpu

F32 = jnp.float32
BF16 = jnp.bfloat16
HIGHEST = lax.Precision.HIGHEST

NORM_EPS = 1e-6
RW_GN_EPS = 64e-5
N_MOD = 9
GRID_W = 64
SHORT_CONV = 5

DN_HEADS = 4
DN_HEAD_DIM = 128
DN_WIDTH = DN_HEADS * DN_HEAD_DIM
DN_AB_PAD = 128
DN_P_WIDTH = 4 * DN_WIDTH + DN_AB_PAD
DN_CHUNK = 128

RW_HEADS = 8
RW_HEAD_DIM = 64
RW_WIDTH = RW_HEADS * RW_HEAD_DIM
RW_LORA = 128
RW_P_WIDTH = 3 * RW_WIDTH + 3 * RW_LORA
RW_CHUNK = 64
RW_PAIR = 2 * RW_HEAD_DIM
RW_PAIRS = RW_WIDTH // RW_PAIR

SCAN_CHUNKS_PER_STEP = 2
HALO = 8
V7X_VMEM_LIMIT = 56 * 1024 * 1024


def _params(*sem):
    return pltpu.CompilerParams(dimension_semantics=sem, vmem_limit_bytes=V7X_VMEM_LIMIT)


def _dot(a, b, precision=None):
    return jnp.dot(a, b, preferred_element_type=F32, precision=precision)


def _dot_nt(a, b):
    return lax.dot_general(a, b, (((1,), (1,)), ((), ())), preferred_element_type=F32)


def _dot_tn(a, b):
    return lax.dot_general(a, b, (((0,), (0,)), ((), ())), preferred_element_type=F32)


def _bdot(a, b):
    return _dot(a.astype(BF16), b.astype(BF16))


def _dot_split3(a01, x):
    x1, x2, x3 = _split3(x)
    return _dot(a01, x1) + _dot(a01, x2) + _dot(a01, x3)


def _split3(x):
    x1 = x.astype(BF16)
    r1 = x - x1.astype(F32)
    x2 = r1.astype(BF16)
    x3 = (r1 - x2.astype(F32)).astype(BF16)
    return x1, x2, x3


def _silu(x):
    return x * jax.nn.sigmoid(x)


def _softplus(x):
    return jnp.maximum(x, 0.0) + jnp.log1p(jnp.exp(-jnp.abs(x)))


def _rms_modulate(h, gain, shift, scale):
    y = h * lax.rsqrt(jnp.mean(h * h, axis=-1, keepdims=True) + NORM_EPS) * gain
    return y * (1.0 + scale) + shift


def _unit_tri_inverse(a, size):
    n = a.shape[0]
    diff = lax.broadcasted_iota(jnp.int32, (n, n), 0) ^ lax.broadcasted_iota(jnp.int32, (n, n), 1)
    t = jnp.where(diff == 0, 1.0, jnp.where(diff == 1, -a, 0.0))
    level = 1
    while (1 << level) < size:
        l = jnp.where((diff >> level) == 1, a, 0.0)
        tl = _bdot(t, l)
        yield
        t = t - _bdot(tl, t)
        yield
        level += 1
    return t


def _chunks_per_step(n_chunks):
    return SCAN_CHUNKS_PER_STEP if n_chunks % SCAN_CHUNKS_PER_STEP == 0 else 1


def _run_lockstep(gens):
    results = [None] * len(gens)
    live = list(enumerate(gens))
    while live:
        still = []
        for i, g in live:
            try:
                next(g)
                still.append((i, g))
            except StopIteration as done:
                results[i] = done.value
        live = still
    return results


def _shift_rows(x, halo_prev, halo_next, s):
    rows = x.shape[0]
    row8 = lax.broadcasted_iota(jnp.int32, (HALO, x.shape[1]), 0)
    back = pltpu.roll(x, s, 0)
    hp = pltpu.roll(halo_prev, s, 0)
    back = jnp.concatenate([jnp.where(row8 < s, hp, back[:HALO]), back[HALO:]], axis=0)
    fwd = pltpu.roll(x, rows - s, 0)
    hn = pltpu.roll(halo_next, HALO - s, 0)
    fwd = jnp.concatenate([fwd[:rows - HALO], jnp.where(row8 >= HALO - s, hn, fwd[rows - HALO:])], axis=0)
    return back, fwd


def _adaln_kernel(s_ref, w_ref, b_ref, o_ref):
    s = _silu(s_ref[...])
    o_ref[0] = _dot(s, w_ref[0], HIGHEST) + b_ref[0]


def _adaln(s_rows, ada_w, ada_b):
    depth, d, n = ada_w.shape
    tn = n // 8
    return pl.pallas_call(
        _adaln_kernel,
        grid=(depth, n // tn),
        in_specs=[
            pl.BlockSpec((8, d), lambda l, j: (0, 0)),
            pl.BlockSpec((1, d, tn), lambda l, j: (l, 0, j)),
            pl.BlockSpec((1, 1, tn), lambda l, j: (l, 0, j)),
        ],
        out_specs=pl.BlockSpec((1, 8, tn), lambda l, j: (l, 0, j)),
        out_shape=jax.ShapeDtypeStruct((depth, 8, n), F32),
        compiler_params=_params("parallel", "parallel"),
        name="adaln",
    )(s_rows, ada_w, ada_b.reshape(depth, 1, n))


def _ffn_kernel(h_ref, mod_ref, g_ref, w1_ref, w3_ref, w2_ref, *rest, row0, final_norm):
    if final_norm:
        fg_ref, o_ref, xn_ref, acc_ref = rest
    else:
        o_ref, xn_ref, acc_ref = rest
    k = pl.program_id(2)

    @pl.when(k == 0)
    def _():
        xn = _rms_modulate(h_ref[0], g_ref[...], mod_ref[0, row0:row0 + 1, :], mod_ref[0, row0 + 1:row0 + 2, :])
        xn_ref[...] = xn.astype(BF16)
        acc_ref[...] = jnp.zeros_like(acc_ref)

    xn = xn_ref[...]
    gate = _dot(xn, w1_ref[...])
    up = _dot(xn, w3_ref[...])
    acc_ref[...] += _dot((_silu(gate) * up).astype(BF16), w2_ref[...])

    @pl.when(k == pl.num_programs(2) - 1)
    def _():
        out = h_ref[0] + 0.5 * mod_ref[0, row0 + 2:row0 + 3, :] * acc_ref[...]
        if final_norm:
            out = out * lax.rsqrt(jnp.mean(out * out, axis=-1, keepdims=True) + NORM_EPS) * fg_ref[...]
        o_ref[0] = out


def _ffn(h, mod, gain, w13, w2, *, row0, tm, final_gain=None):
    bsz, seqlen, d = h.shape
    dff = w2.shape[0]
    tf = dff // 2
    nk = dff // tf
    row = pl.BlockSpec((1, d), lambda b, i, k: (0, 0))
    in_specs = [
        pl.BlockSpec((1, tm, d), lambda b, i, k: (b, i, 0)),
        pl.BlockSpec((1, N_MOD, d), lambda b, i, k: (b, 0, 0)),
        row,
        pl.BlockSpec((d, tf), lambda b, i, k: (0, k)),
        pl.BlockSpec((d, tf), lambda b, i, k: (0, k + nk)),
        pl.BlockSpec((tf, d), lambda b, i, k: (k, 0)),
    ]
    args = [h, mod, gain.reshape(1, d), w13, w13, w2]
    if final_gain is not None:
        in_specs.append(row)
        args.append(final_gain.reshape(1, d))
    return pl.pallas_call(
        functools.partial(_ffn_kernel, row0=row0, final_norm=final_gain is not None),
        grid=(bsz, seqlen // tm, nk),
        in_specs=in_specs,
        out_specs=pl.BlockSpec((1, tm, d), lambda b, i, k: (b, i, 0)),
        out_shape=jax.ShapeDtypeStruct(h.shape, F32),
        scratch_shapes=[pltpu.VMEM((tm, d), BF16), pltpu.VMEM((tm, d), F32)],
        compiler_params=_params("parallel", "parallel", "arbitrary"),
        name="ffn",
    )(*args)


def _proj_kernel(h_ref, mod_ref, g_ref, wdn_ref, wrw_ref, pdn_ref, prw_ref):
    xn = _rms_modulate(h_ref[0], g_ref[...], mod_ref[0, 3:4, :], mod_ref[0, 4:5, :]).astype(BF16)
    pdn_ref[0] = _dot(xn, wdn_ref[...])
    prw_ref[0] = _dot(xn, wrw_ref[...])


def _proj(h, mod, gain, w_dn, w_rw, *, tm):
    bsz, seqlen, d = h.shape
    return pl.pallas_call(
        _proj_kernel,
        grid=(bsz, seqlen // tm),
        in_specs=[
            pl.BlockSpec((1, tm, d), lambda b, i: (b, i, 0)),
            pl.BlockSpec((1, N_MOD, d), lambda b, i: (b, 0, 0)),
            pl.BlockSpec((1, d), lambda b, i: (0, 0)),
            pl.BlockSpec((d, DN_P_WIDTH), lambda b, i: (0, 0)),
            pl.BlockSpec((d, RW_P_WIDTH), lambda b, i: (0, 0)),
        ],
        out_specs=[
            pl.BlockSpec((1, tm, DN_P_WIDTH), lambda b, i: (b, i, 0)),
            pl.BlockSpec((1, tm, RW_P_WIDTH), lambda b, i: (b, i, 0)),
        ],
        out_shape=[
            jax.ShapeDtypeStruct((bsz, seqlen, DN_P_WIDTH), F32),
            jax.ShapeDtypeStruct((bsz, seqlen, RW_P_WIDTH), F32),
        ],
        compiler_params=_params("parallel", "parallel"),
        name="proj",
    )(h, mod, gain.reshape(1, d), w_dn, w_rw)


def _dn_feat_kernel(x_ref, xp_ref, xn_ref, ab_ref, cw_ref, alog_ref, dt_ref, qkv_ref, gb_ref):
    i = pl.program_id(1)
    x = x_ref[0]
    hp = jnp.where(i == 0, 0.0, xp_ref[0])
    hn = jnp.where(i == pl.num_programs(1) - 1, 0.0, xn_ref[0])
    b1, f1 = _shift_rows(x, hp, hn, 1)
    b2, f2 = _shift_rows(x, hp, hn, 2)
    conv = cw_ref[0:1] * b2 + cw_ref[1:2] * b1 + cw_ref[2:3] * x + cw_ref[3:4] * f1 + cw_ref[4:5] * f2
    act = _silu(conv)
    for hd in range(2 * DN_HEADS):
        t = act[:, hd * DN_HEAD_DIM:(hd + 1) * DN_HEAD_DIM]
        t = t * lax.rsqrt(jnp.sum(t * t, axis=-1, keepdims=True) + NORM_EPS)
        if hd < DN_HEADS:
            t = t * DN_HEAD_DIM ** -0.5
        qkv_ref[0, :, hd * DN_HEAD_DIM:(hd + 1) * DN_HEAD_DIM] = t.astype(BF16)
    qkv_ref[0, :, 2 * DN_WIDTH:] = act[:, 2 * DN_WIDTH:].astype(BF16)
    ab = ab_ref[0]
    lane = lax.broadcasted_iota(jnp.int32, ab.shape, 1)
    g = -jnp.exp(alog_ref[...]) * _softplus(ab + dt_ref[...])
    rows = x.shape[0]
    ri = lax.broadcasted_iota(jnp.int32, (rows, rows), 0)
    ci = lax.broadcasted_iota(jnp.int32, (rows, rows), 1)
    same = (ri // DN_CHUNK) == (ci // DN_CHUNK)
    gc_f = _dot_split3(jnp.where(same & (ri >= ci), 1.0, 0.0).astype(BF16), g)
    gc_b = _dot_split3(jnp.where(same & (ri <= ci), 1.0, 0.0).astype(BF16), g)
    gb_ref[0] = jnp.where(lane < DN_HEADS, gc_f, jnp.where(lane < 2 * DN_HEADS, gc_b, jax.nn.sigmoid(ab)))


def _dn_features(p_dn, conv_w, alog_row, dt_row, *, rows):
    bsz, seqlen, _ = p_dn.shape
    nblk = seqlen // rows
    hpb = rows // HALO
    return pl.pallas_call(
        _dn_feat_kernel,
        grid=(bsz, nblk),
        in_specs=[
            pl.BlockSpec((1, rows, 3 * DN_WIDTH), lambda b, i: (b, i, 0)),
            pl.BlockSpec((1, HALO, 3 * DN_WIDTH), lambda b, i: (b, jnp.maximum(i * hpb - 1, 0), 0)),
            pl.BlockSpec((1, HALO, 3 * DN_WIDTH), lambda b, i: (b, jnp.minimum((i + 1) * hpb, nblk * hpb - 1), 0)),
            pl.BlockSpec((1, rows, DN_AB_PAD), lambda b, i: (b, i, 4 * DN_WIDTH // DN_AB_PAD)),
            pl.BlockSpec((8, 3 * DN_WIDTH), lambda b, i: (0, 0)),
            pl.BlockSpec((1, DN_AB_PAD), lambda b, i: (0, 0)),
            pl.BlockSpec((1, DN_AB_PAD), lambda b, i: (0, 0)),
        ],
        out_specs=[
            pl.BlockSpec((1, rows, 3 * DN_WIDTH), lambda b, i: (b, i, 0)),
            pl.BlockSpec((1, rows, DN_AB_PAD), lambda b, i: (b, i, 0)),
        ],
        out_shape=[
            jax.ShapeDtypeStruct((bsz, seqlen, 3 * DN_WIDTH), BF16),
            jax.ShapeDtypeStruct((bsz, seqlen, DN_AB_PAD), F32),
        ],
        compiler_params=_params("parallel", "parallel"),
        name="dn_feat",
    )(p_dn, p_dn, p_dn, p_dn, conv_w, alog_row, dt_row)


def _dn_scan_kernel(*refs, backward, final):
    if final:
        qkv_ref, gb_ref, s0_ref, z_ref, ob_ref, nw_ref, o_ref, sn_ref, st_ref = refs
    else:
        qkv_ref, gb_ref, s0_ref, o_ref, sn_ref, st_ref = refs
    j = pl.program_id(0)

    @pl.when(j == 0)
    def _():
        st_ref[...] = s0_ref[...]

    bsz, c = qkv_ref.shape[0], DN_CHUNK
    n_sub = qkv_ref.shape[1] // c
    ri = lax.broadcasted_iota(jnp.int32, (c, c), 0)
    ci = lax.broadcasted_iota(jnp.int32, (c, c), 1)
    incl = (ri <= ci) if backward else (ri >= ci)
    strict = (ri < ci) if backward else (ri > ci)
    d = 1 if backward else 0
    last = 0 if backward else c - 1

    def chunk(q, k, v, beta, gcc, gcr, gcl, state, turn):
        decay = jnp.where(incl, jnp.exp(jnp.where(incl, gcc - gcr, 0.0)), 0.0)
        kb = k.astype(BF16)
        kq = _dot_nt(jnp.concatenate([kb, q.astype(BF16)], axis=0), kb)
        yield
        a = jnp.where(strict, kq[:c] * decay * beta, 0.0)
        attn = jnp.where(incl, kq[c:] * decay, 0.0)
        t_inv = yield from _unit_tri_inverse(a, c)
        e_gc = jnp.exp(gcc)
        wu = _bdot(t_inv, jnp.concatenate([k * (beta * e_gc), v * beta], axis=1))
        yield
        while state["turn"] != turn:
            yield
        s = state["s"]
        wq = _bdot(jnp.concatenate([wu[:, :DN_HEAD_DIM], q * e_gc], axis=0), s)
        yield
        v_new = wu[:, DN_HEAD_DIM:] - wq[:c]
        o = wq[c:] + _bdot(attn, v_new)
        k_dec = k * jnp.exp(gcl - gcc)
        state["s"] = s * jnp.exp(gcl) + _dot_tn(k_dec.astype(BF16), v_new.astype(BF16))
        state["turn"] = turn + 1
        return o

    heads = [(b, hd) for b in range(bsz) for hd in range(DN_HEADS)]
    head_cols = lambda base, hd: slice(base + hd * DN_HEAD_DIM, base + (hd + 1) * DN_HEAD_DIM)
    states = {key: {"s": st_ref[key[0], key[1]], "turn": 0} for key in heads}
    work, gens = [], []
    for turn in range(n_sub):
        r0 = (n_sub - 1 - turn if backward else turn) * c
        rows = slice(r0, r0 + c)
        gbs = [gb_ref[b, rows, :] for b in range(bsz)]
        gc_rows = [gb.T for gb in gbs]
        for b, hd in heads:
            col = DN_HEADS * d + hd
            work.append((b, hd, rows))
            gens.append(chunk(
                qkv_ref[b, rows, head_cols(0, hd)], qkv_ref[b, rows, head_cols(DN_WIDTH, hd)],
                qkv_ref[b, rows, head_cols(2 * DN_WIDTH, hd)],
                gbs[b][:, 2 * DN_HEADS + col:2 * DN_HEADS + col + 1],
                gbs[b][:, col:col + 1], gc_rows[b][col:col + 1, :], gbs[b][last:last + 1, col:col + 1],
                states[(b, hd)], turn))
    results = _run_lockstep(gens)
    for (b, hd, rows), o in zip(work, results):
        if final:
            o = o + ob_ref[b, rows, head_cols(0, hd)]
            o = o * lax.rsqrt(jnp.mean(o * o, axis=-1, keepdims=True) + NORM_EPS) * nw_ref[...]
            o = o * _silu(z_ref[b, rows, head_cols(0, hd)])
        o_ref[b, rows, head_cols(0, hd)] = o
    for b, hd in heads:
        st_ref[b, hd] = states[(b, hd)]["s"]

    @pl.when(j == pl.num_programs(0) - 1)
    def _():
        sn_ref[...] = st_ref[...]


def _dn_scan(qkv, gb, state0, *, backward, p_dn=None, o_other=None, norm_w=None):
    bsz, seqlen, _ = qkv.shape
    c = DN_CHUNK * _chunks_per_step(seqlen // DN_CHUNK)
    n = seqlen // c
    final = p_dn is not None
    idx = (lambda j: (0, n - 1 - j, 0)) if backward else (lambda j: (0, j, 0))
    st_spec = pl.BlockSpec((bsz, DN_HEADS, DN_HEAD_DIM, DN_HEAD_DIM), lambda j: (0, 0, 0, 0))
    in_specs = [
        pl.BlockSpec((bsz, c, 3 * DN_WIDTH), idx),
        pl.BlockSpec((bsz, c, DN_AB_PAD), idx),
        st_spec,
    ]
    args = [qkv, gb, state0]
    if final:
        zcol = 3 * DN_WIDTH // DN_WIDTH
        in_specs += [
            pl.BlockSpec((bsz, c, DN_WIDTH), lambda j: (0, idx(j)[1], zcol)),
            pl.BlockSpec((bsz, c, DN_WIDTH), idx),
            pl.BlockSpec((1, DN_HEAD_DIM), lambda j: (0, 0)),
        ]
        args += [p_dn, o_other, norm_w.reshape(1, DN_HEAD_DIM)]
    return pl.pallas_call(
        functools.partial(_dn_scan_kernel, backward=backward, final=final),
        grid=(n,),
        in_specs=in_specs,
        out_specs=[pl.BlockSpec((bsz, c, DN_WIDTH), idx), st_spec],
        out_shape=[
            jax.ShapeDtypeStruct((bsz, seqlen, DN_WIDTH), F32),
            jax.ShapeDtypeStruct(state0.shape, F32),
        ],
        scratch_shapes=[pltpu.VMEM(state0.shape, F32)],
        compiler_params=_params("arbitrary"),
        name="dn_scan_bwd" if backward else "dn_scan_fwd",
    )(*args)


def _head_sum(x, ones_bd):
    pieces = _split3(x)
    parts = []
    for p in range(RW_PAIRS):
        sl = slice(p * RW_PAIR, (p + 1) * RW_PAIR)
        parts.append(_dot(pieces[0][:, sl], ones_bd) + _dot(pieces[1][:, sl], ones_bd) + _dot(pieces[2][:, sl], ones_bd))
    return jnp.concatenate(parts, axis=1)


def _pair_ones():
    ri = lax.broadcasted_iota(jnp.int32, (RW_PAIR, RW_PAIR), 0)
    ci = lax.broadcasted_iota(jnp.int32, (RW_PAIR, RW_PAIR), 1)
    return ((ri // RW_HEAD_DIM) == (ci // RW_HEAD_DIM)).astype(F32)


def _rw_feat_kernel(x_ref, xp_ref, xn_ref, mu_ref, w0_ref, w2_ref, a0_ref, a2_ref, g2_ref, kk_ref, ka_ref, u_ref,
                    v_ref, d0_ref, d1_ref, wl0_ref, wl1_ref, fin_ref):
    c = pl.program_id(1)
    x = x_ref[0]
    hp = jnp.where(c == 0, 0.0, xp_ref[0])
    hn = jnp.where(c == pl.num_programs(1) - 1, 0.0, xn_ref[0])
    prev, nxt = _shift_rows(x, hp, hn, 1)
    p = x + mu_ref[0:1] * (prev - x) + mu_ref[1:2] * (nxt - x)
    w = RW_WIDTH
    r, k, v = p[:, :w], p[:, w:2 * w], p[:, 2 * w:3 * w]
    lw = p[:, 3 * w:3 * w + RW_LORA]
    la = p[:, 3 * w + RW_LORA:3 * w + 2 * RW_LORA]
    lg = p[:, 3 * w + 2 * RW_LORA:]
    w_log = -_softplus(-(w0_ref[...] + _bdot(jnp.tanh(lw), w2_ref[...]))) - 0.5
    log_decay = -jnp.exp(w_log)
    a = jax.nn.sigmoid(a0_ref[...] + _bdot(la, a2_ref[...]))
    gate = _bdot(jax.nn.sigmoid(lg), g2_ref[...])
    ones_bd = _pair_ones().astype(BF16)
    kx = k * kk_ref[...]
    kk = kx * lax.rsqrt(_head_sum(kx * kx, ones_bd) + NORM_EPS)
    v_ref[0] = v.astype(BF16)
    rows = x.shape[0]
    n_chunks = rows // RW_CHUNK
    ri = lax.broadcasted_iota(jnp.int32, (rows, rows), 0)
    ci = lax.broadcasted_iota(jnp.int32, (rows, rows), 1)
    same = (ri // RW_CHUNK) == (ci // RW_CHUNK)
    bonus = jnp.zeros_like(v)
    for d, (d_ref, wl_ref) in enumerate(((d0_ref, wl0_ref), (d1_ref, wl1_ref))):
        a_d = a[:, d * w:(d + 1) * w]
        k_dir = k * (1.0 + (a_d - 1.0) * ka_ref[...])
        b_dir = a_d * kk
        bonus = bonus + _head_sum(r * k_dir * u_ref[:, d * w:(d + 1) * w], ones_bd) * v
        ld = log_decay[:, d * w:(d + 1) * w]
        tri = jnp.where(same & ((ri >= ci) if d == 0 else (ri <= ci)), 1.0, 0.0).astype(BF16)
        cum = _dot_split3(tri, ld)
        last_row = lambda q: q * RW_CHUNK + (RW_CHUNK - 1 if d == 0 else 0)
        last = [cum[last_row(q):last_row(q) + 1, :] for q in range(n_chunks)]
        cum_last = jnp.concatenate([jnp.broadcast_to(t, (RW_CHUNK, w)) for t in last], axis=0)
        e_in, e_out, e_tail = jnp.exp(cum), jnp.exp(-cum), jnp.exp(cum_last - cum)
        d_ref[0, :, 0 * w:1 * w] = (r * e_in).astype(BF16)
        d_ref[0, :, 1 * w:2 * w] = (kk * jnp.exp(cum - ld)).astype(BF16)
        d_ref[0, :, 2 * w:3 * w] = (k_dir * e_out).astype(BF16)
        d_ref[0, :, 3 * w:4 * w] = (b_dir * e_out).astype(BF16)
        d_ref[0, :, 4 * w:5 * w] = (k_dir * e_tail).astype(BF16)
        d_ref[0, :, 5 * w:6 * w] = (b_dir * e_tail).astype(BF16)
        wl_ref[0] = jnp.concatenate([jnp.broadcast_to(jnp.exp(t), (8, w)) for t in last], axis=0)
    fin_ref[0, :, :w] = gate
    fin_ref[0, :, w:] = bonus


def _rw_features(p_rw, n_cols, wts):
    bsz, seqlen, _ = p_rw.shape
    rows = seqlen // n_cols
    pv = p_rw.reshape(bsz, rows, n_cols * RW_P_WIDTH)
    hpb = rows // HALO
    const = lambda shape: pl.BlockSpec(shape, lambda b, c: (0, 0))
    w = RW_WIDTH
    out_spec = lambda width: pl.BlockSpec((1, rows, width), lambda b, c: (b, c, 0))
    wl_spec = pl.BlockSpec((1, rows // RW_CHUNK * 8, w), lambda b, c: (b, c, 0))
    return pl.pallas_call(
        _rw_feat_kernel,
        grid=(bsz, n_cols),
        in_specs=[
            pl.BlockSpec((1, rows, RW_P_WIDTH), lambda b, c: (b, 0, c)),
            pl.BlockSpec((1, HALO, RW_P_WIDTH), lambda b, c: (b, hpb - 1, jnp.maximum(c - 1, 0))),
            pl.BlockSpec((1, HALO, RW_P_WIDTH), lambda b, c: (b, 0, jnp.minimum(c + 1, n_cols - 1))),
            const((8, RW_P_WIDTH)),
            const((1, 2 * w)), const((RW_LORA, 2 * w)),
            const((1, 2 * w)), const((RW_LORA, 2 * w)),
            const((RW_LORA, w)),
            const((1, w)), const((1, w)), const((1, 2 * w)),
        ],
        out_specs=[out_spec(w), out_spec(6 * w), out_spec(6 * w), wl_spec, wl_spec, out_spec(2 * w)],
        out_shape=[
            jax.ShapeDtypeStruct((bsz, seqlen, w), BF16),
            jax.ShapeDtypeStruct((bsz, seqlen, 6 * w), BF16),
            jax.ShapeDtypeStruct((bsz, seqlen, 6 * w), BF16),
            jax.ShapeDtypeStruct((bsz, seqlen // RW_CHUNK * 8, w), F32),
            jax.ShapeDtypeStruct((bsz, seqlen // RW_CHUNK * 8, w), F32),
            jax.ShapeDtypeStruct((bsz, seqlen, 2 * w), F32),
        ],
        compiler_params=_params("parallel", "parallel"),
        name="rw_feat",
    )(pv, pv, pv, *wts)


def _stack_heads(x, lane_lo):
    return jnp.concatenate([jnp.where(lane_lo, x, 0.0), jnp.where(lane_lo, 0.0, x)], axis=0)


def _rw_scan_kernel(*refs, backward, final):
    if final:
        v_ref, dr_ref, wl_ref, s0_ref, fin_ref, yb_ref, ln_ref, y_ref, sn_ref, st_ref = refs
    else:
        v_ref, dr_ref, wl_ref, s0_ref, y_ref, sn_ref, st_ref = refs
    j = pl.program_id(0)

    @pl.when(j == 0)
    def _():
        st_ref[...] = s0_ref[...]

    bsz, c = v_ref.shape[0], RW_CHUNK
    n_sub = v_ref.shape[1] // c
    w = RW_WIDTH
    c2 = 2 * c
    ri2 = lax.broadcasted_iota(jnp.int32, (c2, c2), 0)
    ci2 = lax.broadcasted_iota(jnp.int32, (c2, c2), 1)
    incl = (ri2 <= ci2) if backward else (ri2 >= ci2)
    strict = (ri2 < ci2) if backward else (ri2 > ci2)
    pi =lax.broadcasted_iota(jnp.int32, (RW_PAIR, RW_PAIR), 0)
    pj = lax.broadcasted_iota(jnp.int32, (RW_PAIR, RW_PAIR), 1)
    lane_lo = lax.broadcasted_iota(jnp.int32, (c, RW_PAIR), 1) < RW_HEAD_DIM

    def head_mean(t):
        lo = jnp.sum(jnp.where(lane_lo, t, 0.0), axis=-1, keepdims=True)
        hi = jnp.sum(jnp.where(lane_lo, 0.0, t), axis=-1, keepdims=True)
        return jnp.where(lane_lo, lo, hi) * (1.0 / RW_HEAD_DIM)

    def pair_chunk(rts, kts, kis, bis, kds, bds, vs, w_last, state, turn):
        g = _dot_nt(jnp.concatenate([kts, rts], axis=0), jnp.concatenate([bis, kis], axis=0))
        yield
        a_b = jnp.where(strict, g[:c2, :c2], 0.0)
        a_k = jnp.where(strict, g[:c2, c2:], 0.0)
        a_rb = jnp.where(incl, g[c2:, :c2], 0.0)
        a_rk = jnp.where(incl, g[c2:, c2:], 0.0)
        x = _bdot(a_k, vs)
        t_inv = yield from _unit_tri_inverse(a_b, c)
        uk = _bdot(t_inv, jnp.concatenate([x.astype(BF16), kts], axis=1))
        yield
        zeros = jnp.zeros((c2, RW_PAIR), BF16)
        rhs = jnp.concatenate([jnp.concatenate([vs, zeros], axis=1), uk.astype(BF16)], axis=0)
        yr = _bdot(jnp.concatenate([a_rk, -a_rb], axis=1), rhs)
        nm = _dot_tn(jnp.concatenate([kds, -bds], axis=0), rhs)
        yield
        m_full = nm[:, RW_PAIR:] + jnp.where(pi == pj, w_last, 0.0)
        r_hat = rts.astype(F32) + yr[:, RW_PAIR:]
        while state["turn"] != turn:
            yield
        yh = _bdot(jnp.concatenate([r_hat, m_full], axis=0), state["h"])
        ys = yh[:c2] + yr[:, :RW_PAIR]
        state["h"] = yh[c2:] + nm[:, :RW_PAIR]
        state["turn"] = turn + 1
        return ys[:c] + ys[c:]

    pairs = [(b, p) for b in range(bsz) for p in range(RW_PAIRS)]
    states = {key: {"h": st_ref[key[0], key[1]], "turn": 0} for key in pairs}
    work, gens = [], []
    for turn in range(n_sub):
        sub = n_sub - 1 - turn if backward else turn
        rows = slice(sub * c, (sub + 1) * c)
        for b, p in pairs:
            sl = slice(p * RW_PAIR, (p + 1) * RW_PAIR)
            work.append((b, rows, sl))
            operands = [_stack_heads(dr_ref[b, rows, i * w + p * RW_PAIR:i * w + (p + 1) * RW_PAIR], lane_lo)
                        for i in range(6)]
            vs = _stack_heads(v_ref[b, rows, sl], lane_lo)
            gens.append(pair_chunk(*operands, vs, wl_ref[b, 8 * sub:8 * sub + 1, sl], states[(b, p)], turn))
    results = _run_lockstep(gens)
    for (b, rows, sl), y in zip(work, results):
        if final:
            y = y + yb_ref[b, rows, sl] + fin_ref[b, rows, w + sl.start:w + sl.stop]
            yc = y - head_mean(y)
            y = yc * lax.rsqrt(head_mean(yc * yc) + RW_GN_EPS) * ln_ref[0:1, sl] + ln_ref[1:2, sl]
            y = y * fin_ref[b, rows, sl]
        y_ref[b, rows, sl] = y
    for b, p in pairs:
        st_ref[b, p] = states[(b, p)]["h"]

    @pl.when(j == pl.num_programs(0) - 1)
    def _():
        sn_ref[...] = st_ref[...]


def _rw_scan(v, per_dir, w_last, state0, n_cols, *, backward, fin=None, y_other=None, ln=None):
    bsz, seqlen, _ = v.shape
    rows_per_col = seqlen // n_cols
    n_sub = _chunks_per_step(rows_per_col // RW_CHUNK)
    c = RW_CHUNK * n_sub
    n = seqlen // c
    w = RW_WIDTH
    final = fin is not None
    blk = (lambda j: n - 1 - j) if backward else (lambda j: j)
    idx = lambda j: (0, blk(j), 0)
    st_spec = pl.BlockSpec(state0.shape, lambda j: (0, 0, 0, 0))
    in_specs = [pl.BlockSpec((bsz, c, w), idx), pl.BlockSpec((bsz, c, 6 * w), idx),
                pl.BlockSpec((bsz, 8 * n_sub, w), idx), st_spec]
    args = [v, per_dir, w_last, state0]
    if final:
        rows = rows_per_col
        cpc = rows // c
        in_specs += [
            pl.BlockSpec((bsz, c, 2 * w), idx),
            pl.BlockSpec((bsz, c, w), idx),
            pl.BlockSpec((2, w), lambda j: (0, 0)),
        ]
        args += [fin, y_other, ln]
        out_shape_y = jax.ShapeDtypeStruct((bsz, rows, n_cols * w), F32)
        y_spec = pl.BlockSpec((bsz, c, w), lambda j: (0, blk(j) % cpc, blk(j) // cpc))
    else:
        out_shape_y = jax.ShapeDtypeStruct((bsz, seqlen, w), F32)
        y_spec = pl.BlockSpec((bsz, c, w), idx)
    y, state = pl.pallas_call(
        functools.partial(_rw_scan_kernel, backward=backward, final=final),
        grid=(n,),
        in_specs=in_specs,
        out_specs=[y_spec, st_spec],
        out_shape=[out_shape_y, jax.ShapeDtypeStruct(state0.shape, F32)],
        scratch_shapes=[pltpu.VMEM(state0.shape, F32)],
        compiler_params=_params("arbitrary"),
        name="rw_scan_bwd" if backward else "rw_scan_fwd",
    )(*args)
    if final:
        y = y.reshape(bsz, seqlen, w)
    return y, state


def _outproj_kernel(h_ref, mod_ref, dn_ref, rw_ref, wdn_ref, wrw_ref, o_ref):
    mix = _dot(dn_ref[0].astype(BF16), wdn_ref[...]) + _dot(rw_ref[0].astype(BF16), wrw_ref[...])
    o_ref[0] = h_ref[0] + mod_ref[0, 5:6, :] * mix


def _outproj(h, mod, dn, rw, w_out, *, tm):
    bsz, seqlen, d = h.shape
    tok = lambda width: pl.BlockSpec((1, tm, width), lambda b, i: (b, i, 0))
    return pl.pallas_call(
        _outproj_kernel,
        grid=(bsz, seqlen // tm),
        in_specs=[
            tok(d),
            pl.BlockSpec((1, N_MOD, d), lambda b, i: (b, 0, 0)),
            tok(DN_WIDTH), tok(RW_WIDTH),
            pl.BlockSpec((DN_WIDTH, d), lambda b, i: (0, 0)),
            pl.BlockSpec((RW_WIDTH, d), lambda b, i: (1, 0)),
        ],
        out_specs=tok(d),
        out_shape=jax.ShapeDtypeStruct(h.shape, F32),
        compiler_params=_params("parallel", "parallel"),
        name="outproj",
    )(h, mod, dn, rw, w_out, w_out)


def _pad_rows(x, rows):
    return jnp.pad(x, ((0, rows - x.shape[0]), (0, 0)))


def _block_diag2(m):
    z = jnp.zeros_like(m[0])
    return jnp.concatenate([jnp.concatenate([m[0], z], axis=1), jnp.concatenate([z, m[1]], axis=1)], axis=0)


def _token_mixing(h_ctx, h_lat, mod_ctx, mod_lat, gain, lw, *, need_ctx_out, tm_lat, tm_ctx):
    bsz = h_lat.shape[0]
    n_cols = GRID_W
    outs = {}
    dn_state = jnp.zeros((2, bsz, DN_HEADS, DN_HEAD_DIM, DN_HEAD_DIM), F32)
    rw_state = jnp.zeros((2, bsz, RW_PAIRS, RW_PAIR, RW_PAIR), F32)
    dn_states = [dn_state[0], dn_state[1]]
    rw_states = [rw_state[0], rw_state[1]]
    for name, h, mod, tm, cols in (("ctx", h_ctx, mod_ctx, tm_ctx, 1), ("lat", h_lat, mod_lat, tm_lat, n_cols)):
        p_dn, p_rw = _proj(h, mod, gain, lw["w_dn"], lw["w_rw"], tm=min(tm, 512))
        qkv, gb = _dn_features(p_dn, lw["conv_w"], lw["alog_row"], lw["dt_row"], rows=min(512, h.shape[1]))
        o_b, dn_states[1] = _dn_scan(qkv, gb, dn_states[1], backward=True)
        dn_out, dn_states[0] = _dn_scan(qkv, gb, dn_states[0], backward=False, p_dn=p_dn, o_other=o_b,
                                        norm_w=lw["dn_norm"])
        v, dir0, dir1, wl0, wl1, fin = _rw_features(p_rw, cols, lw["rw_feat"])
        y_b, rw_states[1] = _rw_scan(v, dir1, wl1, rw_states[1], cols, backward=True)
        rw_out, rw_states[0] = _rw_scan(v, dir0, wl0, rw_states[0], cols, backward=False, fin=fin, y_other=y_b,
                                        ln=lw["rw_ln"])
        if name == "lat" or need_ctx_out:
            outs[name] = _outproj(h, mod, dn_out, rw_out, lw["w_out"], tm=tm)
    return outs.get("ctx"), outs["lat"]


def kernel(x, c, ctx, c_ctx, ada_w, ada_b, norm_g, ffn_w13, ffn_w2, w_in, dn_conv, dn_a_log, dn_dt_bias, dn_norm,
           rw_mu, rw_w0, rw_w2, rw_a0, rw_a2, rw_g2, rw_kk, rw_ka, rw_u, rw_ln, w_out, final_norm):
    depth = ada_w.shape[0]
    bsz, seqlen, d = x.shape
    ctx_len = ctx.shape[1]
    tm_lat, tm_ctx = 1024, ctx_len

    s_rows = jnp.concatenate([c_ctx[None, :], c, jnp.zeros((8 - 1 - bsz, d), F32)], axis=0)
    mods = _adaln(s_rows, ada_w, ada_b).reshape(depth, 8, N_MOD, d)

    w13_bf, w2_bf = ffn_w13.astype(BF16), ffn_w2.astype(BF16)
    dn_cols = 4 * DN_WIDTH + 4 * DN_HEADS
    w_dn = jnp.pad(w_in[:, :, :dn_cols], ((0, 0), (0, 0), (0, DN_P_WIDTH - dn_cols))).astype(BF16)
    w_rw = w_in[:, :, dn_cols:].astype(BF16)
    w_out_bf = w_out.astype(BF16)
    pad_lanes = lambda v: jnp.pad(v.reshape(1, -1), ((0, 0), (0, DN_AB_PAD - v.size)))

    h_lat, h_ctx = x, ctx
    for i in range(depth):
        last = i == depth - 1
        mod_ctx = jnp.broadcast_to(mods[i, 0:1], (bsz, N_MOD, d))
        mod_lat = mods[i, 1:1 + bsz]
        lw = {
            "w_dn": w_dn[i], "w_rw": w_rw[i], "w_out": w_out_bf[i],
            "conv_w": _pad_rows(dn_conv[i], 8),
            "alog_row": pad_lanes(dn_a_log[i]), "dt_row": pad_lanes(dn_dt_bias[i]),
            "dn_norm": dn_norm[i],
            "rw_feat": (
                _pad_rows(rw_mu[i], 8),
                rw_w0[i].reshape(1, -1), _block_diag2(rw_w2[i]).astype(BF16),
                rw_a0[i].reshape(1, -1), _block_diag2(rw_a2[i]).astype(BF16),
                rw_g2[i].astype(BF16),
                rw_kk[i].reshape(1, -1), rw_ka[i].reshape(1, -1), rw_u[i].reshape(1, -1),
            ),
            "rw_ln": rw_ln[i],
        }
        h_lat = _ffn(h_lat, mod_lat, norm_g[i, 0], w13_bf[i, 0], w2_bf[i, 0], row0=0, tm=tm_lat)
        h_ctx = _ffn(h_ctx, mod_ctx, norm_g[i, 0], w13_bf[i, 0], w2_bf[i, 0], row0=0, tm=tm_ctx)
        h_ctx_mix, h_lat = _token_mixing(h_ctx, h_lat, mod_ctx, mod_lat, norm_g[i, 1], lw,
                                         need_ctx_out=not last, tm_lat=tm_lat, tm_ctx=tm_ctx)
        h_lat = _ffn(h_lat, mod_lat, norm_g[i, 2], w13_bf[i, 1], w2_bf[i, 1], row0=6, tm=tm_lat,
                     final_gain=final_norm if last else None)
        if not last:
            h_ctx = _ffn(h_ctx_mix, mod_ctx, norm_g[i, 2], w13_bf[i, 1], w2_bf[i, 1], row0=6, tm=tm_ctx)
    return h_lat
```

```python
import functools

import jax
import jax.numpy as jnp
from jax import lax
from jax.experimental import pallas as pl
from jax.experimental.pallas import tpu as pltpu

F32 = jnp.float32
BF16 = jnp.bfloat16
HIGHEST = lax.Precision.HIGHEST

NORM_EPS = 1e-6
RW_GN_EPS = 64e-5
N_MOD = 9
GRID_W = 64
SHORT_CONV = 5

DN_HEADS = 4
DN_HEAD_DIM = 128
DN_WIDTH = DN_HEADS * DN_HEAD_DIM
DN_AB_PAD = 128
DN_P_WIDTH = 4 * DN_WIDTH + DN_AB_PAD
DN_CHUNK = 128

RW_HEADS = 8
RW_HEAD_DIM = 64
RW_WIDTH = RW_HEADS * RW_HEAD_DIM
RW_LORA = 128
RW_P_WIDTH = 3 * RW_WIDTH + 3 * RW_LORA
RW_CHUNK = 64
RW_PAIR = 2 * RW_HEAD_DIM
RW_PAIRS = RW_WIDTH // RW_PAIR

SCAN_CHUNKS_PER_STEP = 2
HALO = 8
V7X_VMEM_LIMIT = 56 * 1024 * 1024


def _params(*sem):
    return pltpu.CompilerParams(dimension_semantics=sem, vmem_limit_bytes=V7X_VMEM_LIMIT)


def _dot(a, b, precision=None):
    return jnp.dot(a, b, preferred_element_type=F32, precision=precision)


def _dot_nt(a, b):
    return lax.dot_general(a, b, (((1,), (1,)), ((), ())), preferred_element_type=F32)


def _dot_tn(a, b):
    return lax.dot_general(a, b, (((0,), (0,)), ((), ())), preferred_element_type=F32)


def _bdot(a, b):
    return _dot(a.astype(BF16), b.astype(BF16))


def _dot_split3(a01, x):
    x1, x2, x3 = _split3(x)
    return _dot(a01, x1) + _dot(a01, x2) + _dot(a01, x3)


def _split3(x):
    x1 = x.astype(BF16)
    r1 = x - x1.astype(F32)
    x2 = r1.astype(BF16)
    x3 = (r1 - x2.astype(F32)).astype(BF16)
    return x1, x2, x3


def _silu(x):
    return x * jax.nn.sigmoid(x)


def _softplus(x):
    return jnp.maximum(x, 0.0) + jnp.log1p(jnp.exp(-jnp.abs(x)))


def _rms_modulate(h, gain, shift, scale):
    y = h * lax.rsqrt(jnp.mean(h * h, axis=-1, keepdims=True) + NORM_EPS) * gain
    return y * (1.0 + scale) + shift


def _unit_tri_inverse(a, size):
    n = a.shape[0]
    diff = lax.broadcasted_iota(jnp.int32, (n, n), 0) ^ lax.broadcasted_iota(jnp.int32, (n, n), 1)
    t = jnp.where(diff == 0, 1.0, jnp.where(diff == 1, -a, 0.0))
    level = 1
    while (1 << level) < size:
        l = jnp.where((diff >> level) == 1, a, 0.0)
        tl = _bdot(t, l)
        yield
        t = t - _bdot(tl, t)
        yield
        level += 1
    return t


def _chunks_per_step(n_chunks):
    return SCAN_CHUNKS_PER_STEP if n_chunks % SCAN_CHUNKS_PER_STEP == 0 else 1


def _run_lockstep(gens):
    results = [None] * len(gens)
    live = list(enumerate(gens))
    while live:
        still = []
        for i, g in live:
            try:
                next(g)
                still.append((i, g))
            except StopIteration as done:
                results[i] = done.value
        live = still
    return results


def _shift_rows(x, halo_prev, halo_next, s):
    rows = x.shape[0]
    row8 = lax.broadcasted_iota(jnp.int32, (HALO, x.shape[1]), 0)
    back = pltpu.roll(x, s, 0)
    hp = pltpu.roll(halo_prev, s, 0)
    back = jnp.concatenate([jnp.where(row8 < s, hp, back[:HALO]), back[HALO:]], axis=0)
    fwd = pltpu.roll(x, rows - s, 0)
    hn = pltpu.roll(halo_next, HALO - s, 0)
    fwd = jnp.concatenate([fwd[:rows - HALO], jnp.where(row8 >= HALO - s, hn, fwd[rows - HALO:])], axis=0)
    return back, fwd


def _adaln_kernel(s_ref, w_ref, b_ref, o_ref):
    s = _silu(s_ref[...])
    o_ref[0] = _dot(s, w_ref[0], HIGHEST) + b_ref[0]


def _adaln(s_rows, ada_w, ada_b):
    depth, d, n = ada_w.shape
    tn = n // 8
    return pl.pallas_call(
        _adaln_kernel,
        grid=(depth, n // tn),
        in_specs=[
            pl.BlockSpec((8, d), lambda l, j: (0, 0)),
            pl.BlockSpec((1, d, tn), lambda l, j: (l, 0, j)),
            pl.BlockSpec((1, 1, tn), lambda l, j: (l, 0, j)),
        ],
        out_specs=pl.BlockSpec((1, 8, tn), lambda l, j: (l, 0, j)),
        out_shape=jax.ShapeDtypeStruct((depth, 8, n), F32),
        compiler_params=_params("parallel", "parallel"),
        name="adaln",
    )(s_rows, ada_w, ada_b.reshape(depth, 1, n))


def _ffn_kernel(h_ref, mod_ref, g_ref, w1_ref, w3_ref, w2_ref, *rest, row0, final_norm):
    if final_norm:
        fg_ref, o_ref, xn_ref, acc_ref = rest
    else:
        o_ref, xn_ref, acc_ref = rest
    k = pl.program_id(2)

    @pl.when(k == 0)
    def _():
        xn = _rms_modulate(h_ref[0], g_ref[...], mod_ref[0, row0:row0 + 1, :], mod_ref[0, row0 + 1:row0 + 2, :])
        xn_ref[...] = xn.astype(BF16)
        acc_ref[...] = jnp.zeros_like(acc_ref)

    xn = xn_ref[...]
    gate = _dot(xn, w1_ref[...])
    up = _dot(xn, w3_ref[...])
    acc_ref[...] += _dot((_silu(gate) * up).astype(BF16), w2_ref[...])

    @pl.when(k == pl.num_programs(2) - 1)
    def _():
        out = h_ref[0] + 0.5 * mod_ref[0, row0 + 2:row0 + 3, :] * acc_ref[...]
        if final_norm:
            out = out * lax.rsqrt(jnp.mean(out * out, axis=-1, keepdims=True) + NORM_EPS) * fg_ref[...]
        o_ref[0] = out


def _ffn(h, mod, gain, w13, w2, *, row0, tm, final_gain=None):
    bsz, seqlen, d = h.shape
    dff = w2.shape[0]
    tf = dff // 2
    nk = dff // tf
    row = pl.BlockSpec((1, d), lambda b, i, k: (0, 0))
    in_specs = [
        pl.BlockSpec((1, tm, d), lambda b, i, k: (b, i, 0)),
        pl.BlockSpec((1, N_MOD, d), lambda b, i, k: (b, 0, 0)),
        row,
        pl.BlockSpec((d, tf), lambda b, i, k: (0, k)),
        pl.BlockSpec((d, tf), lambda b, i, k: (0, k + nk)),
        pl.BlockSpec((tf, d), lambda b, i, k: (k, 0)),
    ]
    args = [h, mod, gain.reshape(1, d), w13, w13, w2]
    if final_gain is not None:
        in_specs.append(row)
        args.append(final_gain.reshape(1, d))
    return pl.pallas_call(
        functools.partial(_ffn_kernel, row0=row0, final_norm=final_gain is not None),
        grid=(bsz, seqlen // tm, nk),
        in_specs=in_specs,
        out_specs=pl.BlockSpec((1, tm, d), lambda b, i, k: (b, i, 0)),
        out_shape=jax.ShapeDtypeStruct(h.shape, F32),
        scratch_shapes=[pltpu.VMEM((tm, d), BF16), pltpu.VMEM((tm, d), F32)],
        compiler_params=_params("parallel", "parallel", "arbitrary"),
        name="ffn",
    )(*args)


def _proj_kernel(h_ref, mod_ref, g_ref, wdn_ref, wrw_ref, pdn_ref, prw_ref):
    xn = _rms_modulate(h_ref[0], g_ref[...], mod_ref[0, 3:4, :], mod_ref[0, 4:5, :]).astype(BF16)
    pdn_ref[0] = _dot(xn, wdn_ref[...])
    prw_ref[0] = _dot(xn, wrw_ref[...])


def _proj(h, mod, gain, w_dn, w_rw, *, tm):
    bsz, seqlen, d = h.shape
    return pl.pallas_call(
        _proj_kernel,
        grid=(bsz, seqlen // tm),
        in_specs=[
            pl.BlockSpec((1, tm, d), lambda b, i: (b, i, 0)),
            pl.BlockSpec((1, N_MOD, d), lambda b, i: (b, 0, 0)),
            pl.BlockSpec((1, d), lambda b, i: (0, 0)),
            pl.BlockSpec((d, DN_P_WIDTH), lambda b, i: (0, 0)),
            pl.BlockSpec((d, RW_P_WIDTH), lambda b, i: (0, 0)),
        ],
        out_specs=[
            pl.BlockSpec((1, tm, DN_P_WIDTH), lambda b, i: (b, i, 0)),
            pl.BlockSpec((1, tm, RW_P_WIDTH), lambda b, i: (b, i, 0)),
        ],
        out_shape=[
            jax.ShapeDtypeStruct((bsz, seqlen, DN_P_WIDTH), F32),
            jax.ShapeDtypeStruct((bsz, seqlen, RW_P_WIDTH), F32),
        ],
        compiler_params=_params("parallel", "parallel"),
        name="proj",
    )(h, mod, gain.reshape(1, d), w_dn, w_rw)


def _dn_feat_kernel(x_ref, xp_ref, xn_ref, ab_ref, cw_ref, alog_ref, dt_ref, qkv_ref, gb_ref):
    i = pl.program_id(1)
    x = x_ref[0]
    hp = jnp.where(i == 0, 0.0, xp_ref[0])
    hn = jnp.where(i == pl.num_programs(1) - 1, 0.0, xn_ref[0])
    b1, f1 = _shift_rows(x, hp, hn, 1)
    b2, f2 = _shift_rows(x, hp, hn, 2)
    conv = cw_ref[0:1] * b2 + cw_ref[1:2] * b1 + cw_ref[2:3] * x + cw_ref[3:4] * f1 + cw_ref[4:5] * f2
    act = _silu(conv)
    for hd in range(2 * DN_HEADS):
        t = act[:, hd * DN_HEAD_DIM:(hd + 1) * DN_HEAD_DIM]
        t = t * lax.rsqrt(jnp.sum(t * t, axis=-1, keepdims=True) + NORM_EPS)
        if hd < DN_HEADS:
            t = t * DN_HEAD_DIM ** -0.5
        qkv_ref[0, :, hd * DN_HEAD_DIM:(hd + 1) * DN_HEAD_DIM] = t.astype(BF16)
    qkv_ref[0, :, 2 * DN_WIDTH:] = act[:, 2 * DN_WIDTH:].astype(BF16)
    ab = ab_ref[0]
    lane = lax.broadcasted_iota(jnp.int32, ab.shape, 1)
    g = -jnp.exp(alog_ref[...]) * _softplus(ab + dt_ref[...])
    rows = x.shape[0]
    ri = lax.broadcasted_iota(jnp.int32, (rows, rows), 0)
    ci = lax.broadcasted_iota(jnp.int32, (rows, rows), 1)
    same = (ri // DN_CHUNK) == (ci // DN_CHUNK)
    gc_f = _dot_split3(jnp.where(same & (ri >= ci), 1.0, 0.0).astype(BF16), g)
    gc_b = _dot_split3(jnp.where(same & (ri <= ci), 1.0, 0.0).astype(BF16), g)
    gb_ref[0] = jnp.where(lane < DN_HEADS, gc_f, jnp.where(lane < 2 * DN_HEADS, gc_b, jax.nn.sigmoid(ab)))


def _dn_features(p_dn, conv_w, alog_row, dt_row, *, rows):
    bsz, seqlen, _ = p_dn.shape
    nblk = seqlen // rows
    hpb = rows // HALO
    return pl.pallas_call(
        _dn_feat_kernel,
        grid=(bsz, nblk),
        in_specs=[
            pl.BlockSpec((1, rows, 3 * DN_WIDTH), lambda b, i: (b, i, 0)),
            pl.BlockSpec((1, HALO, 3 * DN_WIDTH), lambda b, i: (b, jnp.maximum(i * hpb - 1, 0), 0)),
            pl.BlockSpec((1, HALO, 3 * DN_WIDTH), lambda b, i: (b, jnp.minimum((i + 1) * hpb, nblk * hpb - 1), 0)),
            pl.BlockSpec((1, rows, DN_AB_PAD), lambda b, i: (b, i, 4 * DN_WIDTH // DN_AB_PAD)),
            pl.BlockSpec((8, 3 * DN_WIDTH), lambda b, i: (0, 0)),
            pl.BlockSpec((1, DN_AB_PAD), lambda b, i: (0, 0)),
            pl.BlockSpec((1, DN_AB_PAD), lambda b, i: (0, 0)),
        ],
        out_specs=[
            pl.BlockSpec((1, rows, 3 * DN_WIDTH), lambda b, i: (b, i, 0)),
            pl.BlockSpec((1, rows, DN_AB_PAD), lambda b, i: (b, i, 0)),
        ],
        out_shape=[
            jax.ShapeDtypeStruct((bsz, seqlen, 3 * DN_WIDTH), BF16),
            jax.ShapeDtypeStruct((bsz, seqlen, DN_AB_PAD), F32),
        ],
        compiler_params=_params("parallel", "parallel"),
        name="dn_feat",
    )(p_dn, p_dn, p_dn, p_dn, conv_w, alog_row, dt_row)


def _dn_scan_kernel(*refs, backward, final):
    if final:
        qkv_ref, gb_ref, s0_ref, z_ref, ob_ref, nw_ref, o_ref, sn_ref, st_ref = refs
    else:
        qkv_ref, gb_ref, s0_ref, o_ref, sn_ref, st_ref = refs
    j = pl.program_id(0)

    @pl.when(j == 0)
    def _():
        st_ref[...] = s0_ref[...]

    bsz, c = qkv_ref.shape[0], DN_CHUNK
    n_sub = qkv_ref.shape[1] // c
    ri = lax.broadcasted_iota(jnp.int32, (c, c), 0)
    ci = lax.broadcasted_iota(jnp.int32, (c, c), 1)
    incl = (ri <= ci) if backward else (ri >= ci)
    strict = (ri < ci) if backward else (ri > ci)
    d = 1 if backward else 0
    last = 0 if backward else c - 1

    def chunk(q, k, v, beta, gcc, gcr, gcl, state, turn):
        decay = jnp.where(incl, jnp.exp(jnp.where(incl, gcc - gcr, 0.0)), 0.0)
        kb = k.astype(BF16)
        kq = _dot_nt(jnp.concatenate([kb, q.astype(BF16)], axis=0), kb)
        yield
        a = jnp.where(strict, kq[:c] * decay * beta, 0.0)
        attn = jnp.where(incl, kq[c:] * decay, 0.0)
        t_inv = yield from _unit_tri_inverse(a, c)
        e_gc = jnp.exp(gcc)
        wu = _bdot(t_inv, jnp.concatenate([k * (beta * e_gc), v * beta], axis=1))
        yield
        while state["turn"] != turn:
            yield
        s = state["s"]
        wq = _bdot(jnp.concatenate([wu[:, :DN_HEAD_DIM], q * e_gc], axis=0), s)
        yield
        v_new = wu[:, DN_HEAD_DIM:] - wq[:c]
        o = wq[c:] + _bdot(attn, v_new)
        k_dec = k * jnp.exp(gcl - gcc)
        state["s"] = s * jnp.exp(gcl) + _dot_tn(k_dec.astype(BF16), v_new.astype(BF16))
        state["turn"] = turn + 1
        return o

    heads = [(b, hd) for b in range(bsz) for hd in range(DN_HEADS)]
    head_cols = lambda base, hd: slice(base + hd * DN_HEAD_DIM, base + (hd + 1) * DN_HEAD_DIM)
    states = {key: {"s": st_ref[key[0], key[1]], "turn": 0} for key in heads}
    work, gens = [], []
    for turn in range(n_sub):
        r0 = (n_sub - 1 - turn if backward else turn) * c
        rows = slice(r0, r0 + c)
        gbs = [gb_ref[b, rows, :] for b in range(bsz)]
        gc_rows = [gb.T for gb in gbs]
        for b, hd in heads:
            col = DN_HEADS * d + hd
            work.append((b, hd, rows))
            gens.append(chunk(
                qkv_ref[b, rows, head_cols(0, hd)], qkv_ref[b, rows, head_cols(DN_WIDTH, hd)],
                qkv_ref[b, rows, head_cols(2 * DN_WIDTH, hd)],
                gbs[b][:, 2 * DN_HEADS + col:2 * DN_HEADS + col + 1],
                gbs[b][:, col:col + 1], gc_rows[b][col:col + 1, :], gbs[b][last:last + 1, col:col + 1],
                states[(b, hd)], turn))
    results = _run_lockstep(gens)
    for (b, hd, rows), o in zip(work, results):
        if final:
            o = o + ob_ref[b, rows, head_cols(0, hd)]
            o = o * lax.rsqrt(jnp.mean(o * o, axis=-1, keepdims=True) + NORM_EPS) * nw_ref[...]
            o = o * _silu(z_ref[b, rows, head_cols(0, hd)])
        o_ref[b, rows, head_cols(0, hd)] = o
    for b, hd in heads:
        st_ref[b, hd] = states[(b, hd)]["s"]

    @pl.when(j == pl.num_programs(0) - 1)
    def _():
        sn_ref[...] = st_ref[...]


def _dn_scan(qkv, gb, state0, *, backward, p_dn=None, o_other=None, norm_w=None):
    bsz, seqlen, _ = qkv.shape
    c = DN_CHUNK * _chunks_per_step(seqlen // DN_CHUNK)
    n = seqlen // c
    final = p_dn is not None
    idx = (lambda j: (0, n - 1 - j, 0)) if backward else (lambda j: (0, j, 0))
    st_spec = pl.BlockSpec((bsz, DN_HEADS, DN_HEAD_DIM, DN_HEAD_DIM), lambda j: (0, 0, 0, 0))
    in_specs = [
        pl.BlockSpec((bsz, c, 3 * DN_WIDTH), idx),
        pl.BlockSpec((bsz, c, DN_AB_PAD), idx),
        st_spec,
    ]
    args = [qkv, gb, state0]
    if final:
        zcol = 3 * DN_WIDTH // DN_WIDTH
        in_specs += [
            pl.BlockSpec((bsz, c, DN_WIDTH), lambda j: (0, idx(j)[1], zcol)),
            pl.BlockSpec((bsz, c, DN_WIDTH), idx),
            pl.BlockSpec((1, DN_HEAD_DIM), lambda j: (0, 0)),
        ]
        args += [p_dn, o_other, norm_w.reshape(1, DN_HEAD_DIM)]
    return pl.pallas_call(
        functools.partial(_dn_scan_kernel, backward=backward, final=final),
        grid=(n,),
        in_specs=in_specs,
        out_specs=[pl.BlockSpec((bsz, c, DN_WIDTH), idx), st_spec],
        out_shape=[
            jax.ShapeDtypeStruct((bsz, seqlen, DN_WIDTH), F32),
            jax.ShapeDtypeStruct(state0.shape, F32),
        ],
        scratch_shapes=[pltpu.VMEM(state0.shape, F32)],
        compiler_params=_params("arbitrary"),
        name="dn_scan_bwd" if backward else "dn_scan_fwd",
    )(*args)


def _head_sum(x, ones_bd):
    pieces = _split3(x)
    parts = []
    for p in range(RW_PAIRS):
        sl = slice(p * RW_PAIR, (p + 1) * RW_PAIR)
        parts.append(_dot(pieces[0][:, sl], ones_bd) + _dot(pieces[1][:, sl], ones_bd) + _dot(pieces[2][:, sl], ones_bd))
    return jnp.concatenate(parts, axis=1)


def _pair_ones():
    ri = lax.broadcasted_iota(jnp.int32, (RW_PAIR, RW_PAIR), 0)
    ci = lax.broadcasted_iota(jnp.int32, (RW_PAIR, RW_PAIR), 1)
    return ((ri // RW_HEAD_DIM) == (ci // RW_HEAD_DIM)).astype(F32)


def _rw_column_copies(p_hbm, xbuf, hbuf, sems, b, c, slot):
    rows, n_cols = p_hbm.shape[1], p_hbm.shape[2]
    return (
        pltpu.make_async_copy(p_hbm.at[b, :, c, :], xbuf.at[slot], sems.at[slot, 0]),
        pltpu.make_async_copy(p_hbm.at[b, pl.ds(rows - HALO, HALO), jnp.maximum(c - 1, 0), :], hbuf.at[slot, 0],
                              sems.at[slot, 1]),
        pltpu.make_async_copy(p_hbm.at[b, pl.ds(0, HALO), jnp.minimum(c + 1, n_cols - 1), :], hbuf.at[slot, 1],
                              sems.at[slot, 2]),
    )


def _rw_feat_kernel(*refs, gather_columns):
    c = pl.program_id(1)
    n_cols = pl.num_programs(1)
    if gather_columns:
        (p_hbm, mu_ref, w0_ref, w2_ref, a0_ref, a2_ref, g2_ref, kk_ref, ka_ref, u_ref,
         v_ref, d0_ref, d1_ref, wl0_ref, wl1_ref, fin_ref, xbuf, hbuf, sems) = refs
        n = pl.program_id(0) * n_cols + c
        slot = n % 2

        @pl.when(n == 0)
        def _():
            for cp in _rw_column_copies(p_hbm, xbuf, hbuf, sems, pl.program_id(0), c, slot):
                cp.start()

        @pl.when(n + 1 < pl.num_programs(0) * n_cols)
        def _():
            for cp in _rw_column_copies(p_hbm, xbuf, hbuf, sems, (n + 1) // n_cols, (n + 1) % n_cols, 1 - slot):
                cp.start()

        for cp in _rw_column_copies(p_hbm, xbuf, hbuf, sems, pl.program_id(0), c, slot):
            cp.wait()
        x, halo_prev, halo_next = xbuf[slot], hbuf[slot, 0], hbuf[slot, 1]
    else:
        (x_ref, xp_ref, xn_ref, mu_ref, w0_ref, w2_ref, a0_ref, a2_ref, g2_ref, kk_ref, ka_ref, u_ref,
         v_ref, d0_ref, d1_ref, wl0_ref, wl1_ref, fin_ref) = refs
        x, halo_prev, halo_next = x_ref[0], xp_ref[0], xn_ref[0]
    hp = jnp.where(c == 0, 0.0, halo_prev)
    hn = jnp.where(c == n_cols - 1, 0.0, halo_next)
    prev, nxt = _shift_rows(x, hp, hn, 1)
    p = x + mu_ref[0:1] * (prev - x) + mu_ref[1:2] * (nxt - x)
    w = RW_WIDTH
    r, k, v = p[:, :w], p[:, w:2 * w], p[:, 2 * w:3 * w]
    lw = p[:, 3 * w:3 * w + RW_LORA]
    la = p[:, 3 * w + RW_LORA:3 * w + 2 * RW_LORA]
    lg = p[:, 3 * w + 2 * RW_LORA:]
    w_log = -_softplus(-(w0_ref[...] + _bdot(jnp.tanh(lw), w2_ref[...]))) - 0.5
    log_decay = -jnp.exp(w_log)
    a = jax.nn.sigmoid(a0_ref[...] + _bdot(la, a2_ref[...]))
    gate = _bdot(jax.nn.sigmoid(lg), g2_ref[...])
    ones_bd = _pair_ones().astype(BF16)
    kx = k * kk_ref[...]
    kk = kx * lax.rsqrt(_head_sum(kx * kx, ones_bd) + NORM_EPS)
    v_ref[0] = v.astype(BF16)
    rows = x.shape[0]
    n_chunks = rows // RW_CHUNK
    ri = lax.broadcasted_iota(jnp.int32, (rows, rows), 0)
    ci = lax.broadcasted_iota(jnp.int32, (rows, rows), 1)
    same = (ri // RW_CHUNK) == (ci // RW_CHUNK)
    bonus = jnp.zeros_like(v)
    for d, (d_ref, wl_ref) in enumerate(((d0_ref, wl0_ref), (d1_ref, wl1_ref))):
        a_d = a[:, d * w:(d + 1) * w]
        k_dir = k * (1.0 + (a_d - 1.0) * ka_ref[...])
        b_dir = a_d * kk
        bonus = bonus + _head_sum(r * k_dir * u_ref[:, d * w:(d + 1) * w], ones_bd) * v
        ld = log_decay[:, d * w:(d + 1) * w]
        tri = jnp.where(same & ((ri >= ci) if d == 0 else (ri <= ci)), 1.0, 0.0).astype(BF16)
        cum = _dot_split3(tri, ld)
        last_row = lambda q: q * RW_CHUNK + (RW_CHUNK - 1 if d == 0 else 0)
        last = [cum[last_row(q):last_row(q) + 1, :] for q in range(n_chunks)]
        cum_last = jnp.concatenate([jnp.broadcast_to(t, (RW_CHUNK, w)) for t in last], axis=0)
        e_in, e_out, e_tail = jnp.exp(cum), jnp.exp(-cum), jnp.exp(cum_last - cum)
        d_ref[0, :, 0 * w:1 * w] = (r * e_in).astype(BF16)
        d_ref[0, :, 1 * w:2 * w] = (kk * jnp.exp(cum - ld)).astype(BF16)
        d_ref[0, :, 2 * w:3 * w] = (k_dir * e_out).astype(BF16)
        d_ref[0, :, 3 * w:4 * w] = (b_dir * e_out).astype(BF16)
        d_ref[0, :, 4 * w:5 * w] = (k_dir * e_tail).astype(BF16)
        d_ref[0, :, 5 * w:6 * w] = (b_dir * e_tail).astype(BF16)
        wl_ref[0] = jnp.concatenate([jnp.broadcast_to(jnp.exp(t), (8, w)) for t in last], axis=0)
    fin_ref[0, :, :w] = gate
    fin_ref[0, :, w:] = bonus


def _rw_features(p_rw, n_cols, wts):
    bsz, seqlen, _ = p_rw.shape
    rows = seqlen // n_cols
    hpb = rows // HALO
    const = lambda shape: pl.BlockSpec(shape, lambda b, c: (0, 0))
    w = RW_WIDTH
    out_spec = lambda width: pl.BlockSpec((1, rows, width), lambda b, c: (b, c, 0))
    wl_spec = pl.BlockSpec((1, rows // RW_CHUNK * 8, w), lambda b, c: (b, c, 0))
    gather_columns = n_cols > 1
    if gather_columns:
        data_specs = [pl.BlockSpec(memory_space=pl.ANY)]
        data = [p_rw.reshape(bsz, rows, n_cols, RW_P_WIDTH)]
        scratch = [pltpu.VMEM((2, rows, RW_P_WIDTH), F32), pltpu.VMEM((2, 2, HALO, RW_P_WIDTH), F32),
                   pltpu.SemaphoreType.DMA((2, 3))]
        semantics = ("arbitrary", "arbitrary")
    else:
        data_specs = [
            pl.BlockSpec((1, rows, RW_P_WIDTH), lambda b, c: (b, 0, 0)),
            pl.BlockSpec((1, HALO, RW_P_WIDTH), lambda b, c: (b, hpb - 1, 0)),
            pl.BlockSpec((1, HALO, RW_P_WIDTH), lambda b, c: (b, 0, 0)),
        ]
        data = [p_rw, p_rw, p_rw]
        scratch = []
        semantics = ("parallel", "parallel")
    return pl.pallas_call(
        functools.partial(_rw_feat_kernel, gather_columns=gather_columns),
        grid=(bsz, n_cols),
        in_specs=data_specs + [
            const((8, RW_P_WIDTH)),
            const((1, 2 * w)), const((RW_LORA, 2 * w)),
            const((1, 2 * w)), const((RW_LORA, 2 * w)),
            const((RW_LORA, w)),
            const((1, w)), const((1, w)), const((1, 2 * w)),
        ],
        scratch_shapes=scratch,
        out_specs=[out_spec(w), out_spec(6 * w), out_spec(6 * w), wl_spec, wl_spec, out_spec(2 * w)],
        out_shape=[
            jax.ShapeDtypeStruct((bsz, seqlen, w), BF16),
            jax.ShapeDtypeStruct((bsz, seqlen, 6 * w), BF16),
            jax.ShapeDtypeStruct((bsz, seqlen, 6 * w), BF16),
            jax.ShapeDtypeStruct((bsz, seqlen // RW_CHUNK * 8, w), F32),
            jax.ShapeDtypeStruct((bsz, seqlen // RW_CHUNK * 8, w), F32),
            jax.ShapeDtypeStruct((bsz, seqlen, 2 * w), F32),
        ],
        compiler_params=_params(*semantics),
        name="rw_feat",
    )(*data, *wts)


def _stack_heads(x, lane_lo):
    return jnp.concatenate([jnp.where(lane_lo, x, 0.0), jnp.where(lane_lo, 0.0, x)], axis=0)


def _rw_scan_kernel(*refs, backward, final, steps_per_column):
    scatter = steps_per_column > 0
    if scatter:
        v_ref, dr_ref, wl_ref, s0_ref, fin_ref, yb_ref, ln_ref, y_hbm, sn_ref, st_ref, y_buf, y_sems = refs
    elif final:
        v_ref, dr_ref, wl_ref, s0_ref, fin_ref, yb_ref, ln_ref, y_ref, sn_ref, st_ref = refs
    else:
        v_ref, dr_ref, wl_ref, s0_ref, y_ref, sn_ref, st_ref = refs
    j = pl.program_id(0)
    n_steps = pl.num_programs(0)

    @pl.when(j == 0)
    def _():
        st_ref[...] = s0_ref[...]

    if scatter:
        slot = j % 2
        step_rows = v_ref.shape[1]

        def out_copy(step, buf_slot):
            r0 = pl.multiple_of((step % steps_per_column) * step_rows, step_rows)
            dst = y_hbm.at[:, pl.ds(r0, step_rows), step // steps_per_column, :]
            return pltpu.make_async_copy(y_buf.at[buf_slot], dst, y_sems.at[buf_slot])

        @pl.when(j >= 2)
        def _():
            out_copy(j - 2, slot).wait()

        y_ref = y_buf.at[slot]

    bsz, c = v_ref.shape[0], RW_CHUNK
    n_sub = v_ref.shape[1] // c
    w = RW_WIDTH
    c2 = 2 * c
    ri2 = lax.broadcasted_iota(jnp.int32, (c2, c2), 0)
    ci2 = lax.broadcasted_iota(jnp.int32, (c2, c2), 1)
    incl = (ri2 <= ci2) if backward else (ri2 >= ci2)
    strict = (ri2 < ci2) if backward else (ri2 > ci2)
    pi =lax.broadcasted_iota(jnp.int32, (RW_PAIR, RW_PAIR), 0)
    pj = lax.broadcasted_iota(jnp.int32, (RW_PAIR, RW_PAIR), 1)
    lane_lo = lax.broadcasted_iota(jnp.int32, (c, RW_PAIR), 1) < RW_HEAD_DIM

    def head_mean(t):
        lo = jnp.sum(jnp.where(lane_lo, t, 0.0), axis=-1, keepdims=True)
        hi = jnp.sum(jnp.where(lane_lo, 0.0, t), axis=-1, keepdims=True)
        return jnp.where(lane_lo, lo, hi) * (1.0 / RW_HEAD_DIM)

    def pair_chunk(rts, kts, kis, bis, kds, bds, vs, w_last, state, turn):
        g = _dot_nt(jnp.concatenate([kts, rts], axis=0), jnp.concatenate([bis, kis], axis=0))
        yield
        a_b = jnp.where(strict, g[:c2, :c2], 0.0)
        a_k = jnp.where(strict, g[:c2, c2:], 0.0)
        a_rb = jnp.where(incl, g[c2:, :c2], 0.0)
        a_rk = jnp.where(incl, g[c2:, c2:], 0.0)
        x = _bdot(a_k, vs)
        t_inv = yield from _unit_tri_inverse(a_b, c)
        uk = _bdot(t_inv, jnp.concatenate([x.astype(BF16), kts], axis=1))
        yield
        zeros = jnp.zeros((c2, RW_PAIR), BF16)
        rhs = jnp.concatenate([jnp.concatenate([vs, zeros], axis=1), uk.astype(BF16)], axis=0)
        yr = _bdot(jnp.concatenate([a_rk, -a_rb], axis=1), rhs)
        nm = _dot_tn(jnp.concatenate([kds, -bds], axis=0), rhs)
        yield
        m_full = nm[:, RW_PAIR:] + jnp.where(pi == pj, w_last, 0.0)
        r_hat = rts.astype(F32) + yr[:, RW_PAIR:]
        while state["turn"] != turn:
            yield
        yh = _bdot(jnp.concatenate([r_hat, m_full], axis=0), state["h"])
        ys = yh[:c2] + yr[:, :RW_PAIR]
        state["h"] = yh[c2:] + nm[:, :RW_PAIR]
        state["turn"] = turn + 1
        return ys[:c] + ys[c:]

    pairs = [(b, p) for b in range(bsz) for p in range(RW_PAIRS)]
    states = {key: {"h": st_ref[key[0], key[1]], "turn": 0} for key in pairs}
    work, gens = [], []
    for turn in range(n_sub):
        sub = n_sub - 1 - turn if backward else turn
        rows = slice(sub * c, (sub + 1) * c)
        for b, p in pairs:
            sl = slice(p * RW_PAIR, (p + 1) * RW_PAIR)
            work.append((b, rows, sl))
            operands = [_stack_heads(dr_ref[b, rows, i * w + p * RW_PAIR:i * w + (p + 1) * RW_PAIR], lane_lo)
                        for i in range(6)]
            vs = _stack_heads(v_ref[b, rows, sl], lane_lo)
            gens.append(pair_chunk(*operands, vs, wl_ref[b, 8 * sub:8 * sub + 1, sl], states[(b, p)], turn))
    results = _run_lockstep(gens)
    for (b, rows, sl), y in zip(work, results):
        if final:
            y = y + yb_ref[b, rows, sl] + fin_ref[b, rows, w + sl.start:w + sl.stop]
            yc = y - head_mean(y)
            y = yc * lax.rsqrt(head_mean(yc * yc) + RW_GN_EPS) * ln_ref[0:1, sl] + ln_ref[1:2, sl]
            y = y * fin_ref[b, rows, sl]
        y_ref[b, rows, sl] = y
    for b, p in pairs:
        st_ref[b, p] = states[(b, p)]["h"]
    if scatter:
        out_copy(j, slot).start()

    @pl.when(j == n_steps - 1)
    def _():
        sn_ref[...] = st_ref[...]
        if scatter:
            out_copy(j, slot).wait()

            @pl.when(j >= 1)
            def _():
                out_copy(j - 1, 1 - slot).wait()


def _rw_scan(v, per_dir, w_last, state0, n_cols, *, backward, fin=None, y_other=None, ln=None):
    bsz, seqlen, _ = v.shape
    rows_per_col = seqlen // n_cols
    n_sub = _chunks_per_step(rows_per_col // RW_CHUNK)
    c = RW_CHUNK * n_sub
    n = seqlen // c
    w = RW_WIDTH
    final = fin is not None
    blk = (lambda j: n - 1 - j) if backward else (lambda j: j)
    idx = lambda j: (0, blk(j), 0)
    st_spec = pl.BlockSpec(state0.shape, lambda j: (0, 0, 0, 0))
    in_specs = [pl.BlockSpec((bsz, c, w), idx), pl.BlockSpec((bsz, c, 6 * w), idx),
                pl.BlockSpec((bsz, 8 * n_sub, w), idx), st_spec]
    args = [v, per_dir, w_last, state0]
    scratch = [pltpu.VMEM(state0.shape, F32)]
    out_shape_y = jax.ShapeDtypeStruct((bsz, seqlen, w), F32)
    y_spec = pl.BlockSpec((bsz, c, w), idx)
    steps_per_column = 0
    if final:
        assert not backward
        in_specs += [
            pl.BlockSpec((bsz, c, 2 * w), idx),
            pl.BlockSpec((bsz, c, w), idx),
            pl.BlockSpec((2, w), lambda j: (0, 0)),
        ]
        args += [fin, y_other, ln]
        if n_cols > 1:
            steps_per_column = rows_per_col // c
            out_shape_y = jax.ShapeDtypeStruct((bsz, rows_per_col, n_cols, w), F32)
            y_spec = pl.BlockSpec(memory_space=pl.ANY)
            scratch += [pltpu.VMEM((2, bsz, c, w), F32), pltpu.SemaphoreType.DMA((2,))]
    y, state = pl.pallas_call(
        functools.partial(_rw_scan_kernel, backward=backward, final=final, steps_per_column=steps_per_column),
        grid=(n,),
        in_specs=in_specs,
        out_specs=[y_spec, st_spec],
        out_shape=[out_shape_y, jax.ShapeDtypeStruct(state0.shape, F32)],
        scratch_shapes=scratch,
        compiler_params=_params("arbitrary"),
        name="rw_scan_bwd" if backward else "rw_scan_fwd",
    )(*args)
    return y.reshape(bsz, seqlen, w), state


def _outproj_kernel(h_ref, mod_ref, dn_ref, rw_ref, wdn_ref, wrw_ref, o_ref):
    mix = _dot(dn_ref[0].astype(BF16), wdn_ref[...]) + _dot(rw_ref[0].astype(BF16), wrw_ref[...])
    o_ref[0] = h_ref[0] + mod_ref[0, 5:6, :] * mix


def _outproj(h, mod, dn, rw, w_out, *, tm):
    bsz, seqlen, d = h.shape
    tok = lambda width: pl.BlockSpec((1, tm, width), lambda b, i: (b, i, 0))
    return pl.pallas_call(
        _outproj_kernel,
        grid=(bsz, seqlen // tm),
        in_specs=[
            tok(d),
            pl.BlockSpec((1, N_MOD, d), lambda b, i: (b, 0, 0)),
            tok(DN_WIDTH), tok(RW_WIDTH),
            pl.BlockSpec((DN_WIDTH, d), lambda b, i: (0, 0)),
            pl.BlockSpec((RW_WIDTH, d), lambda b, i: (1, 0)),
        ],
        out_specs=tok(d),
        out_shape=jax.ShapeDtypeStruct(h.shape, F32),
        compiler_params=_params("parallel", "parallel"),
        name="outproj",
    )(h, mod, dn, rw, w_out, w_out)


def _pad_rows(x, rows):
    return jnp.pad(x, ((0, rows - x.shape[0]), (0, 0)))


def _block_diag2(m):
    z = jnp.zeros_like(m[0])
    return jnp.concatenate([jnp.concatenate([m[0], z], axis=1), jnp.concatenate([z, m[1]], axis=1)], axis=0)


def _token_mixing(h_ctx, h_lat, mod_ctx, mod_lat, gain, lw, *, need_ctx_out, tm_lat, tm_ctx):
    bsz = h_lat.shape[0]
    n_cols = GRID_W
    outs = {}
    dn_state = jnp.zeros((2, bsz, DN_HEADS, DN_HEAD_DIM, DN_HEAD_DIM), F32)
    rw_state = jnp.zeros((2, bsz, RW_PAIRS, RW_PAIR, RW_PAIR), F32)
    dn_states = [dn_state[0], dn_state[1]]
    rw_states = [rw_state[0], rw_state[1]]
    for name, h, mod, tm, cols in (("ctx", h_ctx, mod_ctx, tm_ctx, 1), ("lat", h_lat, mod_lat, tm_lat, n_cols)):
        p_dn, p_rw = _proj(h, mod, gain, lw["w_dn"], lw["w_rw"], tm=min(tm, 512))
        qkv, gb = _dn_features(p_dn, lw["conv_w"], lw["alog_row"], lw["dt_row"], rows=min(512, h.shape[1]))
        o_b, dn_states[1] = _dn_scan(qkv, gb, dn_states[1], backward=True)
        dn_out, dn_states[0] = _dn_scan(qkv, gb, dn_states[0], backward=False, p_dn=p_dn, o_other=o_b,
                                        norm_w=lw["dn_norm"])
        v, dir0, dir1, wl0, wl1, fin = _rw_features(p_rw, cols, lw["rw_feat"])
        y_b, rw_states[1] = _rw_scan(v, dir1, wl1, rw_states[1], cols, backward=True)
        rw_out, rw_states[0] = _rw_scan(v, dir0, wl0, rw_states[0], cols, backward=False, fin=fin, y_other=y_b,
                                        ln=lw["rw_ln"])
        if name == "lat" or need_ctx_out:
            outs[name] = _outproj(h, mod, dn_out, rw_out, lw["w_out"], tm=tm)
    return outs.get("ctx"), outs["lat"]


def kernel(x, c, ctx, c_ctx, ada_w, ada_b, norm_g, ffn_w13, ffn_w2, w_in, dn_conv, dn_a_log, dn_dt_bias, dn_norm,
           rw_mu, rw_w0, rw_w2, rw_a0, rw_a2, rw_g2, rw_kk, rw_ka, rw_u, rw_ln, w_out, final_norm):
    depth = ada_w.shape[0]
    bsz, seqlen, d = x.shape
    ctx_len = ctx.shape[1]
    tm_lat, tm_ctx = 1024, ctx_len

    s_rows = jnp.concatenate([c_ctx[None, :], c, jnp.zeros((8 - 1 - bsz, d), F32)], axis=0)
    mods = _adaln(s_rows, ada_w, ada_b).reshape(depth, 8, N_MOD, d)

    w13_bf, w2_bf = ffn_w13.astype(BF16), ffn_w2.astype(BF16)
    dn_cols = 4 * DN_WIDTH + 4 * DN_HEADS
    w_dn = jnp.pad(w_in[:, :, :dn_cols], ((0, 0), (0, 0), (0, DN_P_WIDTH - dn_cols))).astype(BF16)
    w_rw = w_in[:, :, dn_cols:].astype(BF16)
    w_out_bf = w_out.astype(BF16)
    pad_lanes = lambda v: jnp.pad(v.reshape(1, -1), ((0, 0), (0, DN_AB_PAD - v.size)))

    h_lat, h_ctx = x, ctx
    for i in range(depth):
        last = i == depth - 1
        mod_ctx = jnp.broadcast_to(mods[i, 0:1], (bsz, N_MOD, d))
        mod_lat = mods[i, 1:1 + bsz]
        lw = {
            "w_dn": w_dn[i], "w_rw": w_rw[i], "w_out": w_out_bf[i],
            "conv_w": _pad_rows(dn_conv[i], 8),
            "alog_row": pad_lanes(dn_a_log[i]), "dt_row": pad_lanes(dn_dt_bias[i]),
            "dn_norm": dn_norm[i],
            "rw_feat": (
                _pad_rows(rw_mu[i], 8),
                rw_w0[i].reshape(1, -1), _block_diag2(rw_w2[i]).astype(BF16),
                rw_a0[i].reshape(1, -1), _block_diag2(rw_a2[i]).astype(BF16),
                rw_g2[i].astype(BF16),
                rw_kk[i].reshape(1, -1), rw_ka[i].reshape(1, -1), rw_u[i].reshape(1, -1),
            ),
            "rw_ln": rw_ln[i],
        }
        h_lat = _ffn(h_lat, mod_lat, norm_g[i, 0], w13_bf[i, 0], w2_bf[i, 0], row0=0, tm=tm_lat)
        h_ctx = _ffn(h_ctx, mod_ctx, norm_g[i, 0], w13_bf[i, 0], w2_bf[i, 0], row0=0, tm=tm_ctx)
        h_ctx_mix, h_lat = _token_mixing(h_ctx, h_lat, mod_ctx, mod_lat, norm_g[i, 1], lw,
                                         need_ctx_out=not last, tm_lat=tm_lat, tm_ctx=tm_ctx)
        h_lat = _ffn(h_lat, mod_lat, norm_g[i, 2], w13_bf[i, 1], w2_bf[i, 1], row0=6, tm=tm_lat,
                     final_gain=final_norm if last else None)
        if not last:
            h_ctx = _ffn(h_ctx_mix, mod_ctx, norm_g[i, 2], w13_bf[i, 1], w2_bf[i, 1], row0=6, tm=tm_ctx)
    return h_lat
```

```python
import functools

import jax
import jax.numpy as jnp
from jax import lax
from jax.experimental import pallas as pl
from jax.experimental.pallas import tpu as pltpu

F32 = jnp.float32
BF16 = jnp.bfloat16
HIGHEST = lax.Precision.HIGHEST

NORM_EPS = 1e-6
RW_GN_EPS = 64e-5
N_MOD = 9
GRID_W = 64
SHORT_CONV = 5

DN_HEADS = 4
DN_HEAD_DIM = 128
DN_WIDTH = DN_HEADS * DN_HEAD_DIM
DN_AB_PAD = 128
DN_P_WIDTH = 4 * DN_WIDTH + DN_AB_PAD
DN_CHUNK = 128

RW_HEADS = 8
RW_HEAD_DIM = 64
RW_WIDTH = RW_HEADS * RW_HEAD_DIM
RW_LORA = 128
RW_P_WIDTH = 3 * RW_WIDTH + 3 * RW_LORA
RW_CHUNK = 64
RW_PAIR = 2 * RW_HEAD_DIM
RW_PAIRS = RW_WIDTH // RW_PAIR

SCAN_CHUNKS_PER_STEP = 2
HALO = 8
V7X_VMEM_LIMIT = 56 * 1024 * 1024


def _params(*sem):
    return pltpu.CompilerParams(dimension_semantics=sem, vmem_limit_bytes=V7X_VMEM_LIMIT)


def _dot(a, b, precision=None):
    return jnp.dot(a, b, preferred_element_type=F32, precision=precision)


def _dot_nt(a, b):
    return lax.dot_general(a, b, (((1,), (1,)), ((), ())), preferred_element_type=F32)


def _dot_tn(a, b):
    return lax.dot_general(a, b, (((0,), (0,)), ((), ())), preferred_element_type=F32)


def _bdot(a, b):
    return _dot(a.astype(BF16), b.astype(BF16))


def _dot_split3(a01, x):
    x1, x2, x3 = _split3(x)
    return _dot(a01, x1) + _dot(a01, x2) + _dot(a01, x3)


def _split3(x):
    x1 = x.astype(BF16)
    r1 = x - x1.astype(F32)
    x2 = r1.astype(BF16)
    x3 = (r1 - x2.astype(F32)).astype(BF16)
    return x1, x2, x3


def _silu(x):
    return x * jax.nn.sigmoid(x)


def _softplus(x):
    return jnp.maximum(x, 0.0) + jnp.log1p(jnp.exp(-jnp.abs(x)))


def _rms_modulate(h, gain, shift, scale):
    y = h * lax.rsqrt(jnp.mean(h * h, axis=-1, keepdims=True) + NORM_EPS) * gain
    return y * (1.0 + scale) + shift


def _unit_tri_inverse(a, size):
    n = a.shape[0]
    diff = lax.broadcasted_iota(jnp.int32, (n, n), 0) ^ lax.broadcasted_iota(jnp.int32, (n, n), 1)
    t = jnp.where(diff == 0, 1.0, jnp.where(diff == 1, -a, 0.0))
    level = 1
    while (1 << level) < size:
        l = jnp.where((diff >> level) == 1, a, 0.0)
        tl = _bdot(t, l)
        yield
        t = t - _bdot(tl, t)
        yield
        level += 1
    return t


def _chunks_per_step(n_chunks):
    return SCAN_CHUNKS_PER_STEP if n_chunks % SCAN_CHUNKS_PER_STEP == 0 else 1


def _run_lockstep(gens):
    results = [None] * len(gens)
    live = list(enumerate(gens))
    while live:
        still = []
        for i, g in live:
            try:
                next(g)
                still.append((i, g))
            except StopIteration as done:
                results[i] = done.value
        live = still
    return results


def _shift_rows(x, halo_prev, halo_next, s):
    rows = x.shape[0]
    row8 = lax.broadcasted_iota(jnp.int32, (HALO, x.shape[1]), 0)
    back = pltpu.roll(x, s, 0)
    hp = pltpu.roll(halo_prev, s, 0)
    back = jnp.concatenate([jnp.where(row8 < s, hp, back[:HALO]), back[HALO:]], axis=0)
    fwd = pltpu.roll(x, rows - s, 0)
    hn = pltpu.roll(halo_next, HALO - s, 0)
    fwd = jnp.concatenate([fwd[:rows - HALO], jnp.where(row8 >= HALO - s, hn, fwd[rows - HALO:])], axis=0)
    return back, fwd


def _adaln_kernel(s_ref, w_ref, b_ref, o_ref):
    s = _silu(s_ref[...])
    o_ref[0] = _dot(s, w_ref[0], HIGHEST) + b_ref[0]


def _adaln(s_rows, ada_w, ada_b):
    depth, d, n = ada_w.shape
    tn = n // 8
    return pl.pallas_call(
        _adaln_kernel,
        grid=(depth, n // tn),
        in_specs=[
            pl.BlockSpec((8, d), lambda l, j: (0, 0)),
            pl.BlockSpec((1, d, tn), lambda l, j: (l, 0, j)),
            pl.BlockSpec((1, 1, tn), lambda l, j: (l, 0, j)),
        ],
        out_specs=pl.BlockSpec((1, 8, tn), lambda l, j: (l, 0, j)),
        out_shape=jax.ShapeDtypeStruct((depth, 8, n), F32),
        compiler_params=_params("parallel", "parallel"),
        name="adaln",
    )(s_rows, ada_w, ada_b.reshape(depth, 1, n))


def _ffn_kernel(h_ref, mod_ref, g_ref, w1_ref, w3_ref, w2_ref, *rest, row0, mixed, final_norm):
    rest = list(rest)
    if mixed:
        dn_ref, rw_ref, wdn_ref, wrw_ref = rest[:4]
        rest = rest[4:]
    if final_norm:
        fg_ref = rest.pop(0)
    o_ref, xn_ref, acc_ref = rest[:3]
    res_ref = rest[3] if mixed else None
    k = pl.program_id(2)

    @pl.when(k == 0)
    def _():
        h = h_ref[0]
        if mixed:
            mix = _dot(dn_ref[0].astype(BF16), wdn_ref[...]) + _dot(rw_ref[0].astype(BF16), wrw_ref[...])
            h = h + mod_ref[0, 5:6, :] * mix
            res_ref[...] = h
        xn = _rms_modulate(h, g_ref[...], mod_ref[0, row0:row0 + 1, :], mod_ref[0, row0 + 1:row0 + 2, :])
        xn_ref[...] = xn.astype(BF16)
        acc_ref[...] = jnp.zeros_like(acc_ref)

    xn = xn_ref[...]
    gate = _dot(xn, w1_ref[...])
    up = _dot(xn, w3_ref[...])
    acc_ref[...] += _dot((_silu(gate) * up).astype(BF16), w2_ref[...])

    @pl.when(k == pl.num_programs(2) - 1)
    def _():
        res = res_ref[...] if mixed else h_ref[0]
        out = res + 0.5 * mod_ref[0, row0 + 2:row0 + 3, :] * acc_ref[...]
        if final_norm:
            out = out * lax.rsqrt(jnp.mean(out * out, axis=-1, keepdims=True) + NORM_EPS) * fg_ref[...]
        o_ref[0] = out


def _ffn(h, mod, gain, w13, w2, layer, which, *, row0, tm, mix=None, final_gain=None):
    bsz, seqlen, d = h.shape
    dff = w2.shape[2]
    tf = dff // 2
    nk = dff // tf
    row = pl.BlockSpec((1, d), lambda b, i, k: (0, 0))
    tok = lambda width: pl.BlockSpec((1, tm, width), lambda b, i, k: (b, i, 0))
    in_specs = [
        tok(d),
        pl.BlockSpec((1, N_MOD, d), lambda b, i, k: (b, 0, 0)),
        row,
        pl.BlockSpec((None, None, d, tf), lambda b, i, k: (layer, which, 0, k)),
        pl.BlockSpec((None, None, d, tf), lambda b, i, k: (layer, which, 0, k + nk)),
        pl.BlockSpec((None, None, tf, d), lambda b, i, k: (layer, which, k, 0)),
    ]
    args = [h, mod, gain.reshape(1, d), w13, w13, w2]
    scratch = [pltpu.VMEM((tm, d), BF16), pltpu.VMEM((tm, d), F32)]
    if mix is not None:
        dn, rw, w_out = mix
        in_specs += [
            tok(DN_WIDTH), tok(RW_WIDTH),
            pl.BlockSpec((None, DN_WIDTH, d), lambda b, i, k: (layer, 0, 0)),
            pl.BlockSpec((None, RW_WIDTH, d), lambda b, i, k: (layer, 1, 0)),
        ]
        args += [dn, rw, w_out, w_out]
        scratch.append(pltpu.VMEM((tm, d), F32))
    if final_gain is not None:
        in_specs.append(row)
        args.append(final_gain.reshape(1, d))
    return pl.pallas_call(
        functools.partial(_ffn_kernel, row0=row0, mixed=mix is not None, final_norm=final_gain is not None),
        grid=(bsz, seqlen // tm, nk),
        in_specs=in_specs,
        out_specs=tok(d),
        out_shape=jax.ShapeDtypeStruct(h.shape, F32),
        scratch_shapes=scratch,
        compiler_params=_params("parallel", "parallel", "arbitrary"),
        name="ffn_mix" if mix is not None else "ffn",
    )(*args)


def _proj_kernel(h_ref, mod_ref, g_ref, wdn_ref, wrw_ref, pdn_ref, prw_ref):
    xn = _rms_modulate(h_ref[0], g_ref[...], mod_ref[0, 3:4, :], mod_ref[0, 4:5, :]).astype(BF16)
    pdn_ref[0] = _dot(xn, wdn_ref[...])
    prw_ref[0] = _dot(xn, wrw_ref[...])


def _proj(h, mod, gain, w_dn, w_rw, layer, *, tm):
    bsz, seqlen, d = h.shape
    return pl.pallas_call(
        _proj_kernel,
        grid=(bsz, seqlen // tm),
        in_specs=[
            pl.BlockSpec((1, tm, d), lambda b, i: (b, i, 0)),
            pl.BlockSpec((1, N_MOD, d), lambda b, i: (b, 0, 0)),
            pl.BlockSpec((1, d), lambda b, i: (0, 0)),
            pl.BlockSpec((None, d, DN_P_WIDTH), lambda b, i: (layer, 0, 0)),
            pl.BlockSpec((None, d, RW_P_WIDTH), lambda b, i: (layer, 0, 0)),
        ],
        out_specs=[
            pl.BlockSpec((1, tm, DN_P_WIDTH), lambda b, i: (b, i, 0)),
            pl.BlockSpec((1, tm, RW_P_WIDTH), lambda b, i: (b, i, 0)),
        ],
        out_shape=[
            jax.ShapeDtypeStruct((bsz, seqlen, DN_P_WIDTH), F32),
            jax.ShapeDtypeStruct((bsz, seqlen, RW_P_WIDTH), F32),
        ],
        compiler_params=_params("parallel", "parallel"),
        name="proj",
    )(h, mod, gain.reshape(1, d), w_dn, w_rw)


def _dn_feat_kernel(x_ref, xp_ref, xn_ref, ab_ref, cw_ref, alog_ref, dt_ref, qkv_ref, gb_ref):
    i = pl.program_id(1)
    x = x_ref[0]
    hp = jnp.where(i == 0, 0.0, xp_ref[0])
    hn = jnp.where(i == pl.num_programs(1) - 1, 0.0, xn_ref[0])
    b1, f1 = _shift_rows(x, hp, hn, 1)
    b2, f2 = _shift_rows(x, hp, hn, 2)
    conv = cw_ref[0:1] * b2 + cw_ref[1:2] * b1 + cw_ref[2:3] * x + cw_ref[3:4] * f1 + cw_ref[4:5] * f2
    act = _silu(conv)
    for hd in range(2 * DN_HEADS):
        t = act[:, hd * DN_HEAD_DIM:(hd + 1) * DN_HEAD_DIM]
        t = t * lax.rsqrt(jnp.sum(t * t, axis=-1, keepdims=True) + NORM_EPS)
        if hd < DN_HEADS:
            t = t * DN_HEAD_DIM ** -0.5
        qkv_ref[0, :, hd * DN_HEAD_DIM:(hd + 1) * DN_HEAD_DIM] = t.astype(BF16)
    qkv_ref[0, :, 2 * DN_WIDTH:] = act[:, 2 * DN_WIDTH:].astype(BF16)
    ab = ab_ref[0]
    lane = lax.broadcasted_iota(jnp.int32, ab.shape, 1)
    g = -jnp.exp(alog_ref[...]) * _softplus(ab + dt_ref[...])
    rows = x.shape[0]
    ri = lax.broadcasted_iota(jnp.int32, (rows, rows), 0)
    ci = lax.broadcasted_iota(jnp.int32, (rows, rows), 1)
    same = (ri // DN_CHUNK) == (ci // DN_CHUNK)
    gc_f = _dot_split3(jnp.where(same & (ri >= ci), 1.0, 0.0).astype(BF16), g)
    gc_b = _dot_split3(jnp.where(same & (ri <= ci), 1.0, 0.0).astype(BF16), g)
    gb_ref[0] = jnp.where(lane < DN_HEADS, gc_f, jnp.where(lane < 2 * DN_HEADS, gc_b, jax.nn.sigmoid(ab)))


def _dn_features(p_dn, conv_w, alog_row, dt_row, *, rows):
    bsz, seqlen, _ = p_dn.shape
    nblk = seqlen // rows
    hpb = rows // HALO
    return pl.pallas_call(
        _dn_feat_kernel,
        grid=(bsz, nblk),
        in_specs=[
            pl.BlockSpec((1, rows, 3 * DN_WIDTH), lambda b, i: (b, i, 0)),
            pl.BlockSpec((1, HALO, 3 * DN_WIDTH), lambda b, i: (b, jnp.maximum(i * hpb - 1, 0), 0)),
            pl.BlockSpec((1, HALO, 3 * DN_WIDTH), lambda b, i: (b, jnp.minimum((i + 1) * hpb, nblk * hpb - 1), 0)),
            pl.BlockSpec((1, rows, DN_AB_PAD), lambda b, i: (b, i, 4 * DN_WIDTH // DN_AB_PAD)),
            pl.BlockSpec((8, 3 * DN_WIDTH), lambda b, i: (0, 0)),
            pl.BlockSpec((1, DN_AB_PAD), lambda b, i: (0, 0)),
            pl.BlockSpec((1, DN_AB_PAD), lambda b, i: (0, 0)),
        ],
        out_specs=[
            pl.BlockSpec((1, rows, 3 * DN_WIDTH), lambda b, i: (b, i, 0)),
            pl.BlockSpec((1, rows, DN_AB_PAD), lambda b, i: (b, i, 0)),
        ],
        out_shape=[
            jax.ShapeDtypeStruct((bsz, seqlen, 3 * DN_WIDTH), BF16),
            jax.ShapeDtypeStruct((bsz, seqlen, DN_AB_PAD), F32),
        ],
        compiler_params=_params("parallel", "parallel"),
        name="dn_feat",
    )(p_dn, p_dn, p_dn, p_dn, conv_w, alog_row, dt_row)


def _dn_scan_kernel(*refs, backward, final):
    if final:
        qkv_ref, gb_ref, s0_ref, z_ref, ob_ref, nw_ref, o_ref, sn_ref, st_ref = refs
    else:
        qkv_ref, gb_ref, s0_ref, o_ref, sn_ref, st_ref = refs
    j = pl.program_id(0)

    @pl.when(j == 0)
    def _():
        st_ref[...] = s0_ref[...]

    bsz, c = qkv_ref.shape[0], DN_CHUNK
    n_sub = qkv_ref.shape[1] // c
    ri = lax.broadcasted_iota(jnp.int32, (c, c), 0)
    ci = lax.broadcasted_iota(jnp.int32, (c, c), 1)
    incl = (ri <= ci) if backward else (ri >= ci)
    strict = (ri < ci) if backward else (ri > ci)
    d = 1 if backward else 0
    last = 0 if backward else c - 1

    def chunk(q, k, v, beta, gcc, gcr, gcl, state, turn):
        decay = jnp.where(incl, jnp.exp(jnp.where(incl, gcc - gcr, 0.0)), 0.0)
        kb = k.astype(BF16)
        kq = _dot_nt(jnp.concatenate([kb, q.astype(BF16)], axis=0), kb)
        yield
        a = jnp.where(strict, kq[:c] * decay * beta, 0.0)
        attn = jnp.where(incl, kq[c:] * decay, 0.0)
        t_inv = yield from _unit_tri_inverse(a, c)
        e_gc = jnp.exp(gcc)
        wu = _bdot(t_inv, jnp.concatenate([k * (beta * e_gc), v * beta], axis=1))
        yield
        while state["turn"] != turn:
            yield
        s = state["s"]
        wq = _bdot(jnp.concatenate([wu[:, :DN_HEAD_DIM], q * e_gc], axis=0), s)
        yield
        v_new = wu[:, DN_HEAD_DIM:] - wq[:c]
        o = wq[c:] + _bdot(attn, v_new)
        k_dec = k * jnp.exp(gcl - gcc)
        state["s"] = s * jnp.exp(gcl) + _dot_tn(k_dec.astype(BF16), v_new.astype(BF16))
        state["turn"] = turn + 1
        return o

    heads = [(b, hd) for b in range(bsz) for hd in range(DN_HEADS)]
    head_cols = lambda base, hd: slice(base + hd * DN_HEAD_DIM, base + (hd + 1) * DN_HEAD_DIM)
    states = {key: {"s": st_ref[key[0], key[1]], "turn": 0} for key in heads}
    work, gens = [], []
    for turn in range(n_sub):
        r0 = (n_sub - 1 - turn if backward else turn) * c
        rows = slice(r0, r0 + c)
        gbs = [gb_ref[b, rows, :] for b in range(bsz)]
        gc_rows = [gb.T for gb in gbs]
        for b, hd in heads:
            col = DN_HEADS * d + hd
            work.append((b, hd, rows))
            gens.append(chunk(
                qkv_ref[b, rows, head_cols(0, hd)], qkv_ref[b, rows, head_cols(DN_WIDTH, hd)],
                qkv_ref[b, rows, head_cols(2 * DN_WIDTH, hd)],
                gbs[b][:, 2 * DN_HEADS + col:2 * DN_HEADS + col + 1],
                gbs[b][:, col:col + 1], gc_rows[b][col:col + 1, :], gbs[b][last:last + 1, col:col + 1],
                states[(b, hd)], turn))
    results = _run_lockstep(gens)
    for (b, hd, rows), o in zip(work, results):
        if final:
            o = o + ob_ref[b, rows, head_cols(0, hd)]
            o = o * lax.rsqrt(jnp.mean(o * o, axis=-1, keepdims=True) + NORM_EPS) * nw_ref[...]
            o = o * _silu(z_ref[b, rows, head_cols(0, hd)])
        o_ref[b, rows, head_cols(0, hd)] = o
    for b, hd in heads:
        st_ref[b, hd] = states[(b, hd)]["s"]

    @pl.when(j == pl.num_programs(0) - 1)
    def _():
        sn_ref[...] = st_ref[...]


def _dn_scan(qkv, gb, state0, *, backward, p_dn=None, o_other=None, norm_w=None):
    bsz, seqlen, _ = qkv.shape
    c = DN_CHUNK * _chunks_per_step(seqlen // DN_CHUNK)
    n = seqlen // c
    final = p_dn is not None
    idx = (lambda j: (0, n - 1 - j, 0)) if backward else (lambda j: (0, j, 0))
    st_spec = pl.BlockSpec((bsz, DN_HEADS, DN_HEAD_DIM, DN_HEAD_DIM), lambda j: (0, 0, 0, 0))
    in_specs = [
        pl.BlockSpec((bsz, c, 3 * DN_WIDTH), idx),
        pl.BlockSpec((bsz, c, DN_AB_PAD), idx),
        st_spec,
    ]
    args = [qkv, gb, state0]
    if final:
        zcol = 3 * DN_WIDTH // DN_WIDTH
        in_specs += [
            pl.BlockSpec((bsz, c, DN_WIDTH), lambda j: (0, idx(j)[1], zcol)),
            pl.BlockSpec((bsz, c, DN_WIDTH), idx),
            pl.BlockSpec((1, DN_HEAD_DIM), lambda j: (0, 0)),
        ]
        args += [p_dn, o_other, norm_w.reshape(1, DN_HEAD_DIM)]
    return pl.pallas_call(
        functools.partial(_dn_scan_kernel, backward=backward, final=final),
        grid=(n,),
        in_specs=in_specs,
        out_specs=[pl.BlockSpec((bsz, c, DN_WIDTH), idx), st_spec],
        out_shape=[
            jax.ShapeDtypeStruct((bsz, seqlen, DN_WIDTH), F32),
            jax.ShapeDtypeStruct(state0.shape, F32),
        ],
        scratch_shapes=[pltpu.VMEM(state0.shape, F32)],
        compiler_params=_params("arbitrary"),
        name="dn_scan_bwd" if backward else "dn_scan_fwd",
    )(*args)


def _head_sum(x, ones_bd):
    pieces = _split3(x)
    parts = []
    for p in range(RW_PAIRS):
        sl = slice(p * RW_PAIR, (p + 1) * RW_PAIR)
        parts.append(_dot(pieces[0][:, sl], ones_bd) + _dot(pieces[1][:, sl], ones_bd) + _dot(pieces[2][:, sl], ones_bd))
    return jnp.concatenate(parts, axis=1)


def _pair_ones():
    ri = lax.broadcasted_iota(jnp.int32, (RW_PAIR, RW_PAIR), 0)
    ci = lax.broadcasted_iota(jnp.int32, (RW_PAIR, RW_PAIR), 1)
    return ((ri // RW_HEAD_DIM) == (ci // RW_HEAD_DIM)).astype(F32)


def _rw_column_copies(p_hbm, xbuf, hbuf, sems, b, c, slot):
    rows, n_cols = p_hbm.shape[1], p_hbm.shape[2]
    return (
        pltpu.make_async_copy(p_hbm.at[b, :, c, :], xbuf.at[slot], sems.at[slot, 0]),
        pltpu.make_async_copy(p_hbm.at[b, pl.ds(rows - HALO, HALO), jnp.maximum(c - 1, 0), :], hbuf.at[slot, 0],
                              sems.at[slot, 1]),
        pltpu.make_async_copy(p_hbm.at[b, pl.ds(0, HALO), jnp.minimum(c + 1, n_cols - 1), :], hbuf.at[slot, 1],
                              sems.at[slot, 2]),
    )


def _rw_feat_kernel(*refs, gather_columns):
    c = pl.program_id(1)
    n_cols = pl.num_programs(1)
    if gather_columns:
        (p_hbm, mu_ref, w0_ref, w2_ref, a0_ref, a2_ref, g2_ref, kk_ref, ka_ref, u_ref,
         v_ref, d0_ref, d1_ref, wl0_ref, wl1_ref, fin_ref, xbuf, hbuf, sems) = refs
        n = pl.program_id(0) * n_cols + c
        slot = n % 2

        @pl.when(n == 0)
        def _():
            for cp in _rw_column_copies(p_hbm, xbuf, hbuf, sems, pl.program_id(0), c, slot):
                cp.start()

        @pl.when(n + 1 < pl.num_programs(0) * n_cols)
        def _():
            for cp in _rw_column_copies(p_hbm, xbuf, hbuf, sems, (n + 1) // n_cols, (n + 1) % n_cols, 1 - slot):
                cp.start()

        for cp in _rw_column_copies(p_hbm, xbuf, hbuf, sems, pl.program_id(0), c, slot):
            cp.wait()
        x, halo_prev, halo_next = xbuf[slot], hbuf[slot, 0], hbuf[slot, 1]
    else:
        (x_ref, xp_ref, xn_ref, mu_ref, w0_ref, w2_ref, a0_ref, a2_ref, g2_ref, kk_ref, ka_ref, u_ref,
         v_ref, d0_ref, d1_ref, wl0_ref, wl1_ref, fin_ref) = refs
        x, halo_prev, halo_next = x_ref[0], xp_ref[0], xn_ref[0]
    hp = jnp.where(c == 0, 0.0, halo_prev)
    hn = jnp.where(c == n_cols - 1, 0.0, halo_next)
    prev, nxt = _shift_rows(x, hp, hn, 1)
    p = x + mu_ref[0:1] * (prev - x) + mu_ref[1:2] * (nxt - x)
    w = RW_WIDTH
    r, k, v = p[:, :w], p[:, w:2 * w], p[:, 2 * w:3 * w]
    lw = p[:, 3 * w:3 * w + RW_LORA]
    la = p[:, 3 * w + RW_LORA:3 * w + 2 * RW_LORA]
    lg = p[:, 3 * w + 2 * RW_LORA:]
    w_log = -_softplus(-(w0_ref[...] + _bdot(jnp.tanh(lw), w2_ref[...]))) - 0.5
    log_decay = -jnp.exp(w_log)
    a = jax.nn.sigmoid(a0_ref[...] + _bdot(la, a2_ref[...]))
    gate = _bdot(jax.nn.sigmoid(lg), g2_ref[...])
    ones_bd = _pair_ones().astype(BF16)
    kx = k * kk_ref[...]
    kk = kx * lax.rsqrt(_head_sum(kx * kx, ones_bd) + NORM_EPS)
    v_ref[0] = v.astype(BF16)
    rows = x.shape[0]
    n_chunks = rows // RW_CHUNK
    ri = lax.broadcasted_iota(jnp.int32, (rows, rows), 0)
    ci = lax.broadcasted_iota(jnp.int32, (rows, rows), 1)
    same = (ri // RW_CHUNK) == (ci // RW_CHUNK)
    bonus = jnp.zeros_like(v)
    for d, (d_ref, wl_ref) in enumerate(((d0_ref, wl0_ref), (d1_ref, wl1_ref))):
        a_d = a[:, d * w:(d + 1) * w]
        k_dir = k * (1.0 + (a_d - 1.0) * ka_ref[...])
        b_dir = a_d * kk
        bonus = bonus + _head_sum(r * k_dir * u_ref[:, d * w:(d + 1) * w], ones_bd) * v
        ld = log_decay[:, d * w:(d + 1) * w]
        tri = jnp.where(same & ((ri >= ci) if d == 0 else (ri <= ci)), 1.0, 0.0).astype(BF16)
        cum = _dot_split3(tri, ld)
        last_row = lambda q: q * RW_CHUNK + (RW_CHUNK - 1 if d == 0 else 0)
        last = [cum[last_row(q):last_row(q) + 1, :] for q in range(n_chunks)]
        cum_last = jnp.concatenate([jnp.broadcast_to(t, (RW_CHUNK, w)) for t in last], axis=0)
        e_in, e_out, e_tail = jnp.exp(cum), jnp.exp(-cum), jnp.exp(cum_last - cum)
        d_ref[0, :, 0 * w:1 * w] = (r * e_in).astype(BF16)
        d_ref[0, :, 1 * w:2 * w] = (kk * jnp.exp(cum - ld)).astype(BF16)
        d_ref[0, :, 2 * w:3 * w] = (k_dir * e_out).astype(BF16)
        d_ref[0, :, 3 * w:4 * w] = (b_dir * e_out).astype(BF16)
        d_ref[0, :, 4 * w:5 * w] = (k_dir * e_tail).astype(BF16)
        d_ref[0, :, 5 * w:6 * w] = (b_dir * e_tail).astype(BF16)
        wl_ref[0] = jnp.concatenate([jnp.broadcast_to(jnp.exp(t), (8, w)) for t in last], axis=0)
    fin_ref[0, :, :w] = gate
    fin_ref[0, :, w:] = bonus


def _rw_features(p_rw, n_cols, wts):
    bsz, seqlen, _ = p_rw.shape
    rows = seqlen // n_cols
    hpb = rows // HALO
    const = lambda shape: pl.BlockSpec(shape, lambda b, c: (0, 0))
    w = RW_WIDTH
    out_spec = lambda width: pl.BlockSpec((1, rows, width), lambda b, c: (b, c, 0))
    wl_spec = pl.BlockSpec((1, rows // RW_CHUNK * 8, w), lambda b, c: (b, c, 0))
    gather_columns = n_cols > 1
    if gather_columns:
        data_specs = [pl.BlockSpec(memory_space=pl.ANY)]
        data = [p_rw.reshape(bsz, rows, n_cols, RW_P_WIDTH)]
        scratch = [pltpu.VMEM((2, rows, RW_P_WIDTH), F32), pltpu.VMEM((2, 2, HALO, RW_P_WIDTH), F32),
                   pltpu.SemaphoreType.DMA((2, 3))]
        semantics = ("arbitrary", "arbitrary")
    else:
        data_specs = [
            pl.BlockSpec((1, rows, RW_P_WIDTH), lambda b, c: (b, 0, 0)),
            pl.BlockSpec((1, HALO, RW_P_WIDTH), lambda b, c: (b, hpb - 1, 0)),
            pl.BlockSpec((1, HALO, RW_P_WIDTH), lambda b, c: (b, 0, 0)),
        ]
        data = [p_rw, p_rw, p_rw]
        scratch = []
        semantics = ("parallel", "parallel")
    return pl.pallas_call(
        functools.partial(_rw_feat_kernel, gather_columns=gather_columns),
        grid=(bsz, n_cols),
        in_specs=data_specs + [
            const((8, RW_P_WIDTH)),
            const((1, 2 * w)), const((RW_LORA, 2 * w)),
            const((1, 2 * w)), const((RW_LORA, 2 * w)),
            const((RW_LORA, w)),
            const((1, w)), const((1, w)), const((1, 2 * w)),
        ],
        scratch_shapes=scratch,
        out_specs=[out_spec(w), out_spec(6 * w), out_spec(6 * w), wl_spec, wl_spec, out_spec(2 * w)],
        out_shape=[
            jax.ShapeDtypeStruct((bsz, seqlen, w), BF16),
            jax.ShapeDtypeStruct((bsz, seqlen, 6 * w), BF16),
            jax.ShapeDtypeStruct((bsz, seqlen, 6 * w), BF16),
            jax.ShapeDtypeStruct((bsz, seqlen // RW_CHUNK * 8, w), F32),
            jax.ShapeDtypeStruct((bsz, seqlen // RW_CHUNK * 8, w), F32),
            jax.ShapeDtypeStruct((bsz, seqlen, 2 * w), F32),
        ],
        compiler_params=_params(*semantics),
        name="rw_feat",
    )(*data, *wts)


def _stack_heads(x, lane_lo):
    return jnp.concatenate([jnp.where(lane_lo, x, 0.0), jnp.where(lane_lo, 0.0, x)], axis=0)


def _rw_scan_kernel(*refs, backward, final, steps_per_column):
    scatter = steps_per_column > 0
    if scatter:
        v_ref, dr_ref, wl_ref, s0_ref, fin_ref, yb_ref, ln_ref, y_hbm, sn_ref, st_ref, y_buf, y_sems = refs
    elif final:
        v_ref, dr_ref, wl_ref, s0_ref, fin_ref, yb_ref, ln_ref, y_ref, sn_ref, st_ref = refs
    else:
        v_ref, dr_ref, wl_ref, s0_ref, y_ref, sn_ref, st_ref = refs
    j = pl.program_id(0)
    n_steps = pl.num_programs(0)

    @pl.when(j == 0)
    def _():
        st_ref[...] = s0_ref[...]

    if scatter:
        slot = j % 2
        step_rows = v_ref.shape[1]

        def out_copy(step, buf_slot):
            r0 = pl.multiple_of((step % steps_per_column) * step_rows, step_rows)
            dst = y_hbm.at[:, pl.ds(r0, step_rows), step // steps_per_column, :]
            return pltpu.make_async_copy(y_buf.at[buf_slot], dst, y_sems.at[buf_slot])

        @pl.when(j >= 2)
        def _():
            out_copy(j - 2, slot).wait()

        y_ref = y_buf.at[slot]

    bsz, c = v_ref.shape[0], RW_CHUNK
    n_sub = v_ref.shape[1] // c
    w = RW_WIDTH
    c2 = 2 * c
    ri2 = lax.broadcasted_iota(jnp.int32, (c2, c2), 0)
    ci2 = lax.broadcasted_iota(jnp.int32, (c2, c2), 1)
    incl = (ri2 <= ci2) if backward else (ri2 >= ci2)
    strict = (ri2 < ci2) if backward else (ri2 > ci2)
    pi =lax.broadcasted_iota(jnp.int32, (RW_PAIR, RW_PAIR), 0)
    pj = lax.broadcasted_iota(jnp.int32, (RW_PAIR, RW_PAIR), 1)
    lane_lo = lax.broadcasted_iota(jnp.int32, (c, RW_PAIR), 1) < RW_HEAD_DIM

    def head_mean(t):
        lo = jnp.sum(jnp.where(lane_lo, t, 0.0), axis=-1, keepdims=True)
        hi = jnp.sum(jnp.where(lane_lo, 0.0, t), axis=-1, keepdims=True)
        return jnp.where(lane_lo, lo, hi) * (1.0 / RW_HEAD_DIM)

    def pair_chunk(rts, kts, kis, bis, kds, bds, vs, w_last, state, turn):
        g = _dot_nt(jnp.concatenate([kts, rts], axis=0), jnp.concatenate([bis, kis], axis=0))
        yield
        a_b = jnp.where(strict, g[:c2, :c2], 0.0)
        a_k = jnp.where(strict, g[:c2, c2:], 0.0)
        a_rb = jnp.where(incl, g[c2:, :c2], 0.0)
        a_rk = jnp.where(incl, g[c2:, c2:], 0.0)
        x = _bdot(a_k, vs)
        t_inv = yield from _unit_tri_inverse(a_b, c)
        uk = _bdot(t_inv, jnp.concatenate([x.astype(BF16), kts], axis=1))
        yield
        zeros = jnp.zeros((c2, RW_PAIR), BF16)
        rhs = jnp.concatenate([jnp.concatenate([vs, zeros], axis=1), uk.astype(BF16)], axis=0)
        yr = _bdot(jnp.concatenate([a_rk, -a_rb], axis=1), rhs)
        nm = _dot_tn(jnp.concatenate([kds, -bds], axis=0), rhs)
        yield
        m_full = nm[:, RW_PAIR:] + jnp.where(pi == pj, w_last, 0.0)
        r_hat = rts.astype(F32) + yr[:, RW_PAIR:]
        while state["turn"] != turn:
            yield
        yh = _bdot(jnp.concatenate([r_hat, m_full], axis=0), state["h"])
        ys = yh[:c2] + yr[:, :RW_PAIR]
        state["h"] = yh[c2:] + nm[:, :RW_PAIR]
        state["turn"] = turn + 1
        return ys[:c] + ys[c:]

    pairs = [(b, p) for b in range(bsz) for p in range(RW_PAIRS)]
    states = {key: {"h": st_ref[key[0], key[1]], "turn": 0} for key in pairs}
    work, gens = [], []
    for turn in range(n_sub):
        sub = n_sub - 1 - turn if backward else turn
        rows = slice(sub * c, (sub + 1) * c)
        for b, p in pairs:
            sl = slice(p * RW_PAIR, (p + 1) * RW_PAIR)
            work.append((b, rows, sl))
            operands = [_stack_heads(dr_ref[b, rows, i * w + p * RW_PAIR:i * w + (p + 1) * RW_PAIR], lane_lo)
                        for i in range(6)]
            vs = _stack_heads(v_ref[b, rows, sl], lane_lo)
            gens.append(pair_chunk(*operands, vs, wl_ref[b, 8 * sub:8 * sub + 1, sl], states[(b, p)], turn))
    results = _run_lockstep(gens)
    for (b, rows, sl), y in zip(work, results):
        if final:
            y = y + yb_ref[b, rows, sl] + fin_ref[b, rows, w + sl.start:w + sl.stop]
            yc = y - head_mean(y)
            y = yc * lax.rsqrt(head_mean(yc * yc) + RW_GN_EPS) * ln_ref[0:1, sl] + ln_ref[1:2, sl]
            y = y * fin_ref[b, rows, sl]
        y_ref[b, rows, sl] = y
    for b, p in pairs:
        st_ref[b, p] = states[(b, p)]["h"]
    if scatter:
        out_copy(j, slot).start()

    @pl.when(j == n_steps - 1)
    def _():
        sn_ref[...] = st_ref[...]
        if scatter:
            out_copy(j, slot).wait()

            @pl.when(j >= 1)
            def _():
                out_copy(j - 1, 1 - slot).wait()


def _rw_scan(v, per_dir, w_last, state0, n_cols, *, backward, fin=None, y_other=None, ln=None):
    bsz, seqlen, _ = v.shape
    rows_per_col = seqlen // n_cols
    n_sub = _chunks_per_step(rows_per_col // RW_CHUNK)
    c = RW_CHUNK * n_sub
    n = seqlen // c
    w = RW_WIDTH
    final = fin is not None
    blk = (lambda j: n - 1 - j) if backward else (lambda j: j)
    idx = lambda j: (0, blk(j), 0)
    st_spec = pl.BlockSpec(state0.shape, lambda j: (0, 0, 0, 0))
    in_specs = [pl.BlockSpec((bsz, c, w), idx), pl.BlockSpec((bsz, c, 6 * w), idx),
                pl.BlockSpec((bsz, 8 * n_sub, w), idx), st_spec]
    args = [v, per_dir, w_last, state0]
    scratch = [pltpu.VMEM(state0.shape, F32)]
    out_shape_y = jax.ShapeDtypeStruct((bsz, seqlen, w), F32)
    y_spec = pl.BlockSpec((bsz, c, w), idx)
    steps_per_column = 0
    if final:
        assert not backward
        in_specs += [
            pl.BlockSpec((bsz, c, 2 * w), idx),
            pl.BlockSpec((bsz, c, w), idx),
            pl.BlockSpec((2, w), lambda j: (0, 0)),
        ]
        args += [fin, y_other, ln]
        if n_cols > 1:
            steps_per_column = rows_per_col // c
            out_shape_y = jax.ShapeDtypeStruct((bsz, rows_per_col, n_cols, w), F32)
            y_spec = pl.BlockSpec(memory_space=pl.ANY)
            scratch += [pltpu.VMEM((2, bsz, c, w), F32), pltpu.SemaphoreType.DMA((2,))]
    y, state = pl.pallas_call(
        functools.partial(_rw_scan_kernel, backward=backward, final=final, steps_per_column=steps_per_column),
        grid=(n,),
        in_specs=in_specs,
        out_specs=[y_spec, st_spec],
        out_shape=[out_shape_y, jax.ShapeDtypeStruct(state0.shape, F32)],
        scratch_shapes=scratch,
        compiler_params=_params("arbitrary"),
        name="rw_scan_bwd" if backward else "rw_scan_fwd",
    )(*args)
    return y.reshape(bsz, seqlen, w), state


def _pad_rows(x, rows):
    return jnp.pad(x, ((0, rows - x.shape[0]), (0, 0)))


def _block_diag2(m):
    z = jnp.zeros_like(m[0])
    return jnp.concatenate([jnp.concatenate([m[0], z], axis=1), jnp.concatenate([z, m[1]], axis=1)], axis=0)


def _token_mixing(h_ctx, h_lat, mod_ctx, mod_lat, gain, lw, layer, *, tm_lat, tm_ctx):
    bsz = h_lat.shape[0]
    n_cols = GRID_W
    outs = []
    dn_state = jnp.zeros((2, bsz, DN_HEADS, DN_HEAD_DIM, DN_HEAD_DIM), F32)
    rw_state = jnp.zeros((2, bsz, RW_PAIRS, RW_PAIR, RW_PAIR), F32)
    dn_states = [dn_state[0], dn_state[1]]
    rw_states = [rw_state[0], rw_state[1]]
    for h, mod, tm, cols in ((h_ctx, mod_ctx, tm_ctx, 1), (h_lat, mod_lat, tm_lat, n_cols)):
        p_dn, p_rw = _proj(h, mod, gain, lw["w_dn"], lw["w_rw"], layer, tm=min(tm, 512))
        qkv, gb = _dn_features(p_dn, lw["conv_w"], lw["alog_row"], lw["dt_row"], rows=min(512, h.shape[1]))
        o_b, dn_states[1] = _dn_scan(qkv, gb, dn_states[1], backward=True)
        dn_out, dn_states[0] = _dn_scan(qkv, gb, dn_states[0], backward=False, p_dn=p_dn, o_other=o_b,
                                        norm_w=lw["dn_norm"])
        v, dir0, dir1, wl0, wl1, fin = _rw_features(p_rw, cols, lw["rw_feat"])
        y_b, rw_states[1] = _rw_scan(v, dir1, wl1, rw_states[1], cols, backward=True)
        rw_out, rw_states[0] = _rw_scan(v, dir0, wl0, rw_states[0], cols, backward=False, fin=fin, y_other=y_b,
                                        ln=lw["rw_ln"])
        outs.append((dn_out, rw_out))
    return outs


def kernel(x, c, ctx, c_ctx, ada_w, ada_b, norm_g, ffn_w13, ffn_w2, w_in, dn_conv, dn_a_log, dn_dt_bias, dn_norm,
           rw_mu, rw_w0, rw_w2, rw_a0, rw_a2, rw_g2, rw_kk, rw_ka, rw_u, rw_ln, w_out, final_norm):
    depth = ada_w.shape[0]
    bsz, seqlen, d = x.shape
    ctx_len = ctx.shape[1]
    tm_lat, tm_ctx = 1024, ctx_len
    tm_mix = 512

    s_rows = jnp.concatenate([c_ctx[None, :], c, jnp.zeros((8 - 1 - bsz, d), F32)], axis=0)
    mods = _adaln(s_rows, ada_w, ada_b).reshape(depth, 8, N_MOD, d)

    w13_bf, w2_bf = ffn_w13.astype(BF16), ffn_w2.astype(BF16)
    dn_cols = 4 * DN_WIDTH + 4 * DN_HEADS
    w_dn = jnp.pad(w_in[:, :, :dn_cols], ((0, 0), (0, 0), (0, DN_P_WIDTH - dn_cols))).astype(BF16)
    w_rw = w_in[:, :, dn_cols:].astype(BF16)
    w_out_bf = w_out.astype(BF16)
    pad_lanes = lambda v: jnp.pad(v.reshape(1, -1), ((0, 0), (0, DN_AB_PAD - v.size)))

    h_lat, h_ctx = x, ctx
    for i in range(depth):
        last = i == depth - 1
        mod_ctx = jnp.broadcast_to(mods[i, 0:1], (bsz, N_MOD, d))
        mod_lat = mods[i, 1:1 + bsz]
        lw = {
            "w_dn": w_dn, "w_rw": w_rw,
            "conv_w": _pad_rows(dn_conv[i], 8),
            "alog_row": pad_lanes(dn_a_log[i]), "dt_row": pad_lanes(dn_dt_bias[i]),
            "dn_norm": dn_norm[i],
            "rw_feat": (
                _pad_rows(rw_mu[i], 8),
                rw_w0[i].reshape(1, -1), _block_diag2(rw_w2[i]).astype(BF16),
                rw_a0[i].reshape(1, -1), _block_diag2(rw_a2[i]).astype(BF16),
                rw_g2[i].astype(BF16),
                rw_kk[i].reshape(1, -1), rw_ka[i].reshape(1, -1), rw_u[i].reshape(1, -1),
            ),
            "rw_ln": rw_ln[i],
        }
        h_lat = _ffn(h_lat, mod_lat, norm_g[i, 0], w13_bf, w2_bf, i, 0, row0=0, tm=tm_lat)
        h_ctx = _ffn(h_ctx, mod_ctx, norm_g[i, 0], w13_bf, w2_bf, i, 0, row0=0, tm=tm_ctx)
        mix_ctx, mix_lat = _token_mixing(h_ctx, h_lat, mod_ctx, mod_lat, norm_g[i, 1], lw, i,
                                         tm_lat=tm_lat, tm_ctx=tm_ctx)
        h_lat = _ffn(h_lat, mod_lat, norm_g[i, 2], w13_bf, w2_bf, i, 1, row0=6, tm=tm_mix,
                     mix=(*mix_lat, w_out_bf), final_gain=final_norm if last else None)
        if not last:
            h_ctx = _ffn(h_ctx, mod_ctx, norm_g[i, 2], w13_bf, w2_bf, i, 1, row0=6, tm=tm_ctx,
                         mix=(*mix_ctx, w_out_bf))
    return h_lat
```

```python
import functools

import jax
import jax.numpy as jnp
from jax import lax
from jax.experimental import pallas as pl
from jax.experimental.pallas import tpu as pltpu

F32 = jnp.float32
BF16 = jnp.bfloat16
HIGHEST = lax.Precision.HIGHEST

NORM_EPS = 1e-6
RW_GN_EPS = 64e-5
N_MOD = 9
GRID_W = 64
SHORT_CONV = 5

DN_HEADS = 4
DN_HEAD_DIM = 128
DN_WIDTH = DN_HEADS * DN_HEAD_DIM
DN_AB_PAD = 128
DN_P_WIDTH = 4 * DN_WIDTH + DN_AB_PAD
DN_CHUNK = 128

RW_HEADS = 8
RW_HEAD_DIM = 64
RW_WIDTH = RW_HEADS * RW_HEAD_DIM
RW_LORA = 128
RW_P_WIDTH = 3 * RW_WIDTH + 3 * RW_LORA
RW_CHUNK = 64
RW_PAIR = 2 * RW_HEAD_DIM
RW_PAIRS = RW_WIDTH // RW_PAIR

SCAN_CHUNKS_PER_STEP = 2
HALO = 8
V7X_VMEM_LIMIT = 56 * 1024 * 1024


def _params(*sem):
    return pltpu.CompilerParams(dimension_semantics=sem, vmem_limit_bytes=V7X_VMEM_LIMIT)


def _dot(a, b, precision=None):
    return jnp.dot(a, b, preferred_element_type=F32, precision=precision)


def _dot_nt(a, b):
    return lax.dot_general(a, b, (((1,), (1,)), ((), ())), preferred_element_type=F32)


def _dot_tn(a, b):
    return lax.dot_general(a, b, (((0,), (0,)), ((), ())), preferred_element_type=F32)


def _bdot(a, b):
    return _dot(a.astype(BF16), b.astype(BF16))


def _dot_split3(a01, x):
    x1, x2, x3 = _split3(x)
    return _dot(a01, x1) + _dot(a01, x2) + _dot(a01, x3)


def _split3(x):
    x1 = x.astype(BF16)
    r1 = x - x1.astype(F32)
    x2 = r1.astype(BF16)
    x3 = (r1 - x2.astype(F32)).astype(BF16)
    return x1, x2, x3


def _silu(x):
    return x * jax.nn.sigmoid(x)


def _softplus(x):
    return jnp.maximum(x, 0.0) + jnp.log1p(jnp.exp(-jnp.abs(x)))


def _rms_modulate(h, gain, shift, scale):
    y = h * lax.rsqrt(jnp.mean(h * h, axis=-1, keepdims=True) + NORM_EPS) * gain
    return y * (1.0 + scale) + shift


def _unit_tri_inverse(a, size):
    n = a.shape[0]
    diff = lax.broadcasted_iota(jnp.int32, (n, n), 0) ^ lax.broadcasted_iota(jnp.int32, (n, n), 1)
    t = jnp.where(diff == 0, 1.0, jnp.where(diff == 1, -a, 0.0))
    level = 1
    while (1 << level) < size:
        l = jnp.where((diff >> level) == 1, a, 0.0)
        tl = _bdot(t, l)
        yield
        t = t - _bdot(tl, t)
        yield
        level += 1
    return t


def _chunks_per_step(n_chunks):
    return SCAN_CHUNKS_PER_STEP if n_chunks % SCAN_CHUNKS_PER_STEP == 0 else 1


def _run_lockstep(gens):
    results = [None] * len(gens)
    live = list(enumerate(gens))
    while live:
        still = []
        for i, g in live:
            try:
                next(g)
                still.append((i, g))
            except StopIteration as done:
                results[i] = done.value
        live = still
    return results


def _shift_rows(x, halo_prev, halo_next, s):
    rows = x.shape[0]
    row8 = lax.broadcasted_iota(jnp.int32, (HALO, x.shape[1]), 0)
    back = pltpu.roll(x, s, 0)
    hp = pltpu.roll(halo_prev, s, 0)
    back = jnp.concatenate([jnp.where(row8 < s, hp, back[:HALO]), back[HALO:]], axis=0)
    fwd = pltpu.roll(x, rows - s, 0)
    hn = pltpu.roll(halo_next, HALO - s, 0)
    fwd = jnp.concatenate([fwd[:rows - HALO], jnp.where(row8 >= HALO - s, hn, fwd[rows - HALO:])], axis=0)
    return back, fwd


def _adaln_kernel(s_ref, w_ref, b_ref, o_ref):
    s = _silu(s_ref[...])
    o_ref[0] = _dot(s, w_ref[0], HIGHEST) + b_ref[0]


def _adaln(s_rows, ada_w, ada_b):
    depth, d, n = ada_w.shape
    tn = n // 8
    return pl.pallas_call(
        _adaln_kernel,
        grid=(depth, n // tn),
        in_specs=[
            pl.BlockSpec((8, d), lambda l, j: (0, 0)),
            pl.BlockSpec((1, d, tn), lambda l, j: (l, 0, j)),
            pl.BlockSpec((1, 1, tn), lambda l, j: (l, 0, j)),
        ],
        out_specs=pl.BlockSpec((1, 8, tn), lambda l, j: (l, 0, j)),
        out_shape=jax.ShapeDtypeStruct((depth, 8, n), F32),
        compiler_params=_params("parallel", "parallel"),
        name="adaln",
    )(s_rows, ada_w, ada_b.reshape(depth, 1, n))


def _ffn_kernel(h_ref, mod_ref, g_ref, w1_ref, w3_ref, w2_ref, *rest, row0, mixed, final_norm):
    rest = list(rest)
    if mixed:
        dn_ref, rw_ref, wdn_ref, wrw_ref = rest[:4]
        rest = rest[4:]
    if final_norm:
        fg_ref = rest.pop(0)
    o_ref, xn_ref, acc_ref = rest[:3]
    res_ref = rest[3] if mixed else None
    k = pl.program_id(2)

    @pl.when(k == 0)
    def _():
        h = h_ref[0]
        if mixed:
            mix = _dot(dn_ref[0].astype(BF16), wdn_ref[...]) + _dot(rw_ref[0].astype(BF16), wrw_ref[...])
            h = h + mod_ref[0, 5:6, :] * mix
            res_ref[...] = h
        xn = _rms_modulate(h, g_ref[...], mod_ref[0, row0:row0 + 1, :], mod_ref[0, row0 + 1:row0 + 2, :])
        xn_ref[...] = xn.astype(BF16)
        acc_ref[...] = jnp.zeros_like(acc_ref)

    xn = xn_ref[...]
    gate = _dot(xn, w1_ref[...])
    up = _dot(xn, w3_ref[...])
    acc_ref[...] += _dot((_silu(gate) * up).astype(BF16), w2_ref[...])

    @pl.when(k == pl.num_programs(2) - 1)
    def _():
        res = res_ref[...] if mixed else h_ref[0]
        out = res + 0.5 * mod_ref[0, row0 + 2:row0 + 3, :] * acc_ref[...]
        if final_norm:
            out = out * lax.rsqrt(jnp.mean(out * out, axis=-1, keepdims=True) + NORM_EPS) * fg_ref[...]
        o_ref[0] = out


def _ffn(h, mod, gain, w13, w2, layer, which, *, row0, tm, mix=None, final_gain=None):
    bsz, seqlen, d = h.shape
    dff = w2.shape[2]
    tf = dff // 2
    nk = dff // tf
    row = pl.BlockSpec((1, d), lambda b, i, k: (0, 0))
    tok = lambda width: pl.BlockSpec((1, tm, width), lambda b, i, k: (b, i, 0))
    in_specs = [
        tok(d),
        pl.BlockSpec((1, N_MOD, d), lambda b, i, k: (b, 0, 0)),
        row,
        pl.BlockSpec((None, None, d, tf), lambda b, i, k: (layer, which, 0, k)),
        pl.BlockSpec((None, None, d, tf), lambda b, i, k: (layer, which, 0, k + nk)),
        pl.BlockSpec((None, None, tf, d), lambda b, i, k: (layer, which, k, 0)),
    ]
    args = [h, mod, gain.reshape(1, d), w13, w13, w2]
    scratch = [pltpu.VMEM((tm, d), BF16), pltpu.VMEM((tm, d), F32)]
    if mix is not None:
        dn, rw, w_out = mix
        in_specs += [
            tok(DN_WIDTH), tok(RW_WIDTH),
            pl.BlockSpec((None, DN_WIDTH, d), lambda b, i, k: (layer, 0, 0)),
            pl.BlockSpec((None, RW_WIDTH, d), lambda b, i, k: (layer, 1, 0)),
        ]
        args += [dn, rw, w_out, w_out]
        scratch.append(pltpu.VMEM((tm, d), F32))
    if final_gain is not None:
        in_specs.append(row)
        args.append(final_gain.reshape(1, d))
    return pl.pallas_call(
        functools.partial(_ffn_kernel, row0=row0, mixed=mix is not None, final_norm=final_gain is not None),
        grid=(bsz, seqlen // tm, nk),
        in_specs=in_specs,
        out_specs=tok(d),
        out_shape=jax.ShapeDtypeStruct(h.shape, F32),
        scratch_shapes=scratch,
        compiler_params=_params("parallel", "parallel", "arbitrary"),
        name="ffn_mix" if mix is not None else "ffn",
    )(*args)


def _proj_kernel(h_ref, mod_ref, g_ref, wdn_ref, wrw_ref, pdn_ref, prw_ref):
    xn = _rms_modulate(h_ref[0], g_ref[...], mod_ref[0, 3:4, :], mod_ref[0, 4:5, :]).astype(BF16)
    pdn_ref[0] = _dot(xn, wdn_ref[...])
    prw_ref[0] = _dot(xn, wrw_ref[...])


def _proj(h, mod, gain, w_dn, w_rw, layer, *, tm):
    bsz, seqlen, d = h.shape
    return pl.pallas_call(
        _proj_kernel,
        grid=(bsz, seqlen // tm),
        in_specs=[
            pl.BlockSpec((1, tm, d), lambda b, i: (b, i, 0)),
            pl.BlockSpec((1, N_MOD, d), lambda b, i: (b, 0, 0)),
            pl.BlockSpec((1, d), lambda b, i: (0, 0)),
            pl.BlockSpec((None, d, DN_P_WIDTH), lambda b, i: (layer, 0, 0)),
            pl.BlockSpec((None, d, RW_P_WIDTH), lambda b, i: (layer, 0, 0)),
        ],
        out_specs=[
            pl.BlockSpec((1, tm, DN_P_WIDTH), lambda b, i: (b, i, 0)),
            pl.BlockSpec((1, tm, RW_P_WIDTH), lambda b, i: (b, i, 0)),
        ],
        out_shape=[
            jax.ShapeDtypeStruct((bsz, seqlen, DN_P_WIDTH), F32),
            jax.ShapeDtypeStruct((bsz, seqlen, RW_P_WIDTH), F32),
        ],
        compiler_params=_params("parallel", "parallel"),
        name="proj",
    )(h, mod, gain.reshape(1, d), w_dn, w_rw)


def _dn_feat_kernel(x_ref, xp_ref, xn_ref, ab_ref, cw_ref, alog_ref, dt_ref, qkv_ref, gb_ref):
    i = pl.program_id(1)
    x = x_ref[0]
    hp = jnp.where(i == 0, 0.0, xp_ref[0])
    hn = jnp.where(i == pl.num_programs(1) - 1, 0.0, xn_ref[0])
    b1, f1 = _shift_rows(x, hp, hn, 1)
    b2, f2 = _shift_rows(x, hp, hn, 2)
    conv = cw_ref[0:1] * b2 + cw_ref[1:2] * b1 + cw_ref[2:3] * x + cw_ref[3:4] * f1 + cw_ref[4:5] * f2
    act = _silu(conv)
    for hd in range(2 * DN_HEADS):
        t = act[:, hd * DN_HEAD_DIM:(hd + 1) * DN_HEAD_DIM]
        t = t * lax.rsqrt(jnp.sum(t * t, axis=-1, keepdims=True) + NORM_EPS)
        if hd < DN_HEADS:
            t = t * DN_HEAD_DIM ** -0.5
        qkv_ref[0, :, hd * DN_HEAD_DIM:(hd + 1) * DN_HEAD_DIM] = t.astype(BF16)
    qkv_ref[0, :, 2 * DN_WIDTH:] = act[:, 2 * DN_WIDTH:].astype(BF16)
    ab = ab_ref[0]
    lane = lax.broadcasted_iota(jnp.int32, ab.shape, 1)
    g = -jnp.exp(alog_ref[...]) * _softplus(ab + dt_ref[...])
    rows = x.shape[0]
    ri = lax.broadcasted_iota(jnp.int32, (rows, rows), 0)
    ci = lax.broadcasted_iota(jnp.int32, (rows, rows), 1)
    same = (ri // DN_CHUNK) == (ci // DN_CHUNK)
    gc_f = _dot_split3(jnp.where(same & (ri >= ci), 1.0, 0.0).astype(BF16), g)
    gc_b = _dot_split3(jnp.where(same & (ri <= ci), 1.0, 0.0).astype(BF16), g)
    gb_ref[0] = jnp.where(lane < DN_HEADS, gc_f, jnp.where(lane < 2 * DN_HEADS, gc_b, jax.nn.sigmoid(ab)))


def _dn_features(p_dn, conv_w, alog_row, dt_row, *, rows):
    bsz, seqlen, _ = p_dn.shape
    nblk = seqlen // rows
    hpb = rows // HALO
    return pl.pallas_call(
        _dn_feat_kernel,
        grid=(bsz, nblk),
        in_specs=[
            pl.BlockSpec((1, rows, 3 * DN_WIDTH), lambda b, i: (b, i, 0)),
            pl.BlockSpec((1, HALO, 3 * DN_WIDTH), lambda b, i: (b, jnp.maximum(i * hpb - 1, 0), 0)),
            pl.BlockSpec((1, HALO, 3 * DN_WIDTH), lambda b, i: (b, jnp.minimum((i + 1) * hpb, nblk * hpb - 1), 0)),
            pl.BlockSpec((1, rows, DN_AB_PAD), lambda b, i: (b, i, 4 * DN_WIDTH // DN_AB_PAD)),
            pl.BlockSpec((8, 3 * DN_WIDTH), lambda b, i: (0, 0)),
            pl.BlockSpec((1, DN_AB_PAD), lambda b, i: (0, 0)),
            pl.BlockSpec((1, DN_AB_PAD), lambda b, i: (0, 0)),
        ],
        out_specs=[
            pl.BlockSpec((1, rows, 3 * DN_WIDTH), lambda b, i: (b, i, 0)),
            pl.BlockSpec((1, rows, DN_AB_PAD), lambda b, i: (b, i, 0)),
        ],
        out_shape=[
            jax.ShapeDtypeStruct((bsz, seqlen, 3 * DN_WIDTH), BF16),
            jax.ShapeDtypeStruct((bsz, seqlen, DN_AB_PAD), F32),
        ],
        compiler_params=_params("parallel", "parallel"),
        name="dn_feat",
    )(p_dn, p_dn, p_dn, p_dn, conv_w, alog_row, dt_row)


def _dn_scan_kernel(*refs, backward, final):
    if final:
        qkv_ref, gb_ref, s0_ref, z_ref, ob_ref, nw_ref, o_ref, sn_ref, st_ref = refs
    else:
        qkv_ref, gb_ref, s0_ref, o_ref, sn_ref, st_ref = refs
    j = pl.program_id(0)

    @pl.when(j == 0)
    def _():
        st_ref[...] = s0_ref[...]

    bsz, c = qkv_ref.shape[0], DN_CHUNK
    n_sub = qkv_ref.shape[1] // c
    ri = lax.broadcasted_iota(jnp.int32, (c, c), 0)
    ci = lax.broadcasted_iota(jnp.int32, (c, c), 1)
    incl = (ri <= ci) if backward else (ri >= ci)
    strict = (ri < ci) if backward else (ri > ci)
    d = 1 if backward else 0
    last = 0 if backward else c - 1

    def chunk(q, k, v, beta, gcc, gcr, gcl, state, turn):
        decay = jnp.where(incl, jnp.exp(jnp.where(incl, gcc - gcr, 0.0)), 0.0)
        kb = k.astype(BF16)
        kq = _dot_nt(jnp.concatenate([kb, q.astype(BF16)], axis=0), kb)
        yield
        a = jnp.where(strict, kq[:c] * decay * beta, 0.0)
        attn = jnp.where(incl, kq[c:] * decay, 0.0)
        t_inv = yield from _unit_tri_inverse(a, c)
        e_gc = jnp.exp(gcc)
        wu = _bdot(t_inv, jnp.concatenate([k * (beta * e_gc), v * beta], axis=1))
        yield
        while state["turn"] != turn:
            yield
        s = state["s"]
        wq = _bdot(jnp.concatenate([wu[:, :DN_HEAD_DIM], q * e_gc], axis=0), s)
        yield
        v_new = wu[:, DN_HEAD_DIM:] - wq[:c]
        o = wq[c:] + _bdot(attn, v_new)
        k_dec = k * jnp.exp(gcl - gcc)
        state["s"] = s * jnp.exp(gcl) + _dot_tn(k_dec.astype(BF16), v_new.astype(BF16))
        state["turn"] = turn + 1
        return o

    heads = [(b, hd) for b in range(bsz) for hd in range(DN_HEADS)]
    head_cols = lambda base, hd: slice(base + hd * DN_HEAD_DIM, base + (hd + 1) * DN_HEAD_DIM)
    states = {key: {"s": st_ref[key[0], key[1]], "turn": 0} for key in heads}
    work, gens = [], []
    for turn in range(n_sub):
        r0 = (n_sub - 1 - turn if backward else turn) * c
        rows = slice(r0, r0 + c)
        gbs = [gb_ref[b, rows, :] for b in range(bsz)]
        gc_rows = [gb.T for gb in gbs]
        for b, hd in heads:
            col = DN_HEADS * d + hd
            work.append((b, hd, rows))
            gens.append(chunk(
                qkv_ref[b, rows, head_cols(0, hd)], qkv_ref[b, rows, head_cols(DN_WIDTH, hd)],
                qkv_ref[b, rows, head_cols(2 * DN_WIDTH, hd)],
                gbs[b][:, 2 * DN_HEADS + col:2 * DN_HEADS + col + 1],
                gbs[b][:, col:col + 1], gc_rows[b][col:col + 1, :], gbs[b][last:last + 1, col:col + 1],
                states[(b, hd)], turn))
    results = _run_lockstep(gens)
    for (b, hd, rows), o in zip(work, results):
        if final:
            o = o + ob_ref[b, rows, head_cols(0, hd)]
            o = o * lax.rsqrt(jnp.mean(o * o, axis=-1, keepdims=True) + NORM_EPS) * nw_ref[...]
            o = o * _silu(z_ref[b, rows, head_cols(0, hd)])
        o_ref[b, rows, head_cols(0, hd)] = o
    for b, hd in heads:
        st_ref[b, hd] = states[(b, hd)]["s"]

    @pl.when(j == pl.num_programs(0) - 1)
    def _():
        sn_ref[...] = st_ref[...]


def _dn_scan(qkv, gb, state0, *, backward, p_dn=None, o_other=None, norm_w=None):
    bsz, seqlen, _ = qkv.shape
    c = DN_CHUNK * _chunks_per_step(seqlen // DN_CHUNK)
    n = seqlen // c
    final = p_dn is not None
    idx = (lambda j: (0, n - 1 - j, 0)) if backward else (lambda j: (0, j, 0))
    st_spec = pl.BlockSpec((bsz, DN_HEADS, DN_HEAD_DIM, DN_HEAD_DIM), lambda j: (0, 0, 0, 0))
    in_specs = [
        pl.BlockSpec((bsz, c, 3 * DN_WIDTH), idx),
        pl.BlockSpec((bsz, c, DN_AB_PAD), idx),
        st_spec,
    ]
    args = [qkv, gb, state0]
    if final:
        zcol = 3 * DN_WIDTH // DN_WIDTH
        in_specs += [
            pl.BlockSpec((bsz, c, DN_WIDTH), lambda j: (0, idx(j)[1], zcol)),
            pl.BlockSpec((bsz, c, DN_WIDTH), idx),
            pl.BlockSpec((1, DN_HEAD_DIM), lambda j: (0, 0)),
        ]
        args += [p_dn, o_other, norm_w.reshape(1, DN_HEAD_DIM)]
    return pl.pallas_call(
        functools.partial(_dn_scan_kernel, backward=backward, final=final),
        grid=(n,),
        in_specs=in_specs,
        out_specs=[pl.BlockSpec((bsz, c, DN_WIDTH), idx), st_spec],
        out_shape=[
            jax.ShapeDtypeStruct((bsz, seqlen, DN_WIDTH), F32),
            jax.ShapeDtypeStruct(state0.shape, F32),
        ],
        scratch_shapes=[pltpu.VMEM(state0.shape, F32)],
        compiler_params=_params("arbitrary"),
        name="dn_scan_bwd" if backward else "dn_scan_fwd",
    )(*args)


def _head_sum(x, ones_bd):
    pieces = _split3(x)
    parts = []
    for p in range(RW_PAIRS):
        sl = slice(p * RW_PAIR, (p + 1) * RW_PAIR)
        parts.append(_dot(pieces[0][:, sl], ones_bd) + _dot(pieces[1][:, sl], ones_bd) + _dot(pieces[2][:, sl], ones_bd))
    return jnp.concatenate(parts, axis=1)


def _pair_ones():
    ri = lax.broadcasted_iota(jnp.int32, (RW_PAIR, RW_PAIR), 0)
    ci = lax.broadcasted_iota(jnp.int32, (RW_PAIR, RW_PAIR), 1)
    return ((ri // RW_HEAD_DIM) == (ci // RW_HEAD_DIM)).astype(F32)


def _rw_column_copies(p_hbm, xbuf, hbuf, sems, b, c, slot):
    rows, n_cols = p_hbm.shape[1], p_hbm.shape[2]
    return (
        pltpu.make_async_copy(p_hbm.at[b, :, c, :], xbuf.at[slot], sems.at[slot, 0]),
        pltpu.make_async_copy(p_hbm.at[b, pl.ds(rows - HALO, HALO), jnp.maximum(c - 1, 0), :], hbuf.at[slot, 0],
                              sems.at[slot, 1]),
        pltpu.make_async_copy(p_hbm.at[b, pl.ds(0, HALO), jnp.minimum(c + 1, n_cols - 1), :], hbuf.at[slot, 1],
                              sems.at[slot, 2]),
    )


def _rw_feat_kernel(*refs, gather_columns):
    c = pl.program_id(1)
    n_cols = pl.num_programs(1)
    if gather_columns:
        (p_hbm, mu_ref, w0_ref, w2_ref, a0_ref, a2_ref, g2_ref, kk_ref, ka_ref, u_ref,
         sh_ref, kb0_ref, kb1_ref, cm0_ref, cm1_ref, fin_ref, xbuf, hbuf, sems) = refs
        n = pl.program_id(0) * n_cols + c
        slot = n % 2

        @pl.when(n == 0)
        def _():
            for cp in _rw_column_copies(p_hbm, xbuf, hbuf, sems, pl.program_id(0), c, slot):
                cp.start()

        @pl.when(n + 1 < pl.num_programs(0) * n_cols)
        def _():
            for cp in _rw_column_copies(p_hbm, xbuf, hbuf, sems, (n + 1) // n_cols, (n + 1) % n_cols, 1 - slot):
                cp.start()

        for cp in _rw_column_copies(p_hbm, xbuf, hbuf, sems, pl.program_id(0), c, slot):
            cp.wait()
        x, halo_prev, halo_next = xbuf[slot], hbuf[slot, 0], hbuf[slot, 1]
    else:
        (x_ref, xp_ref, xn_ref, mu_ref, w0_ref, w2_ref, a0_ref, a2_ref, g2_ref, kk_ref, ka_ref, u_ref,
         sh_ref, kb0_ref, kb1_ref, cm0_ref, cm1_ref, fin_ref) = refs
        x, halo_prev, halo_next = x_ref[0], xp_ref[0], xn_ref[0]
    hp = jnp.where(c == 0, 0.0, halo_prev)
    hn = jnp.where(c == n_cols - 1, 0.0, halo_next)
    prev, nxt = _shift_rows(x, hp, hn, 1)
    p = x + mu_ref[0:1] * (prev - x) + mu_ref[1:2] * (nxt - x)
    w = RW_WIDTH
    r, k, v = p[:, :w], p[:, w:2 * w], p[:, 2 * w:3 * w]
    lw = p[:, 3 * w:3 * w + RW_LORA]
    la = p[:, 3 * w + RW_LORA:3 * w + 2 * RW_LORA]
    lg = p[:, 3 * w + 2 * RW_LORA:]
    w_log = -_softplus(-(w0_ref[...] + _bdot(jnp.tanh(lw), w2_ref[...]))) - 0.5
    log_decay = -jnp.exp(w_log)
    a = jax.nn.sigmoid(a0_ref[...] + _bdot(la, a2_ref[...]))
    gate = _bdot(jax.nn.sigmoid(lg), g2_ref[...])
    ones_bd = _pair_ones().astype(BF16)
    kx = k * kk_ref[...]
    kk = kx * lax.rsqrt(_head_sum(kx * kx, ones_bd) + NORM_EPS)
    sh_ref[0, :, 0 * w:1 * w] = r.astype(BF16)
    sh_ref[0, :, 1 * w:2 * w] = kk.astype(BF16)
    sh_ref[0, :, 2 * w:3 * w] = v.astype(BF16)
    rows = x.shape[0]
    ri = lax.broadcasted_iota(jnp.int32, (rows, rows), 0)
    ci = lax.broadcasted_iota(jnp.int32, (rows, rows), 1)
    same = (ri // RW_CHUNK) == (ci // RW_CHUNK)
    bonus = jnp.zeros_like(v)
    for d, (kb_ref, cm_ref) in enumerate(((kb0_ref, cm0_ref), (kb1_ref, cm1_ref))):
        a_d = a[:, d * w:(d + 1) * w]
        k_dir = k * (1.0 + (a_d - 1.0) * ka_ref[...])
        bonus = bonus + _head_sum(r * k_dir * u_ref[:, d * w:(d + 1) * w], ones_bd) * v
        ld = log_decay[:, d * w:(d + 1) * w]
        kb_ref[0, :, 0 * w:1 * w] = k_dir.astype(BF16)
        kb_ref[0, :, 1 * w:2 * w] = (a_d * kk).astype(BF16)
        kb_ref[0, :, 2 * w:3 * w] = ld.astype(BF16)
        tri = jnp.where(same & ((ri >= ci) if d == 0 else (ri <= ci)), 1.0, 0.0).astype(BF16)
        cm_ref[0] = _dot_split3(tri, ld)
    fin_ref[0, :, :w] = gate
    fin_ref[0, :, w:] = bonus


def _rw_features(p_rw, n_cols, wts):
    bsz, seqlen, _ = p_rw.shape
    rows = seqlen // n_cols
    hpb = rows // HALO
    const = lambda shape: pl.BlockSpec(shape, lambda b, c: (0, 0))
    w = RW_WIDTH
    out_spec = lambda width: pl.BlockSpec((1, rows, width), lambda b, c: (b, c, 0))
    gather_columns = n_cols > 1
    if gather_columns:
        data_specs = [pl.BlockSpec(memory_space=pl.ANY)]
        data = [p_rw.reshape(bsz, rows, n_cols, RW_P_WIDTH)]
        scratch = [pltpu.VMEM((2, rows, RW_P_WIDTH), F32), pltpu.VMEM((2, 2, HALO, RW_P_WIDTH), F32),
                   pltpu.SemaphoreType.DMA((2, 3))]
        semantics = ("arbitrary", "arbitrary")
    else:
        data_specs = [
            pl.BlockSpec((1, rows, RW_P_WIDTH), lambda b, c: (b, 0, 0)),
            pl.BlockSpec((1, HALO, RW_P_WIDTH), lambda b, c: (b, hpb - 1, 0)),
            pl.BlockSpec((1, HALO, RW_P_WIDTH), lambda b, c: (b, 0, 0)),
        ]
        data = [p_rw, p_rw, p_rw]
        scratch = []
        semantics = ("parallel", "parallel")
    return pl.pallas_call(
        functools.partial(_rw_feat_kernel, gather_columns=gather_columns),
        grid=(bsz, n_cols),
        in_specs=data_specs + [
            const((8, RW_P_WIDTH)),
            const((1, 2 * w)), const((RW_LORA, 2 * w)),
            const((1, 2 * w)), const((RW_LORA, 2 * w)),
            const((RW_LORA, w)),
            const((1, w)), const((1, w)), const((1, 2 * w)),
        ],
        scratch_shapes=scratch,
        out_specs=[out_spec(3 * w), out_spec(3 * w), out_spec(3 * w), out_spec(w), out_spec(w), out_spec(2 * w)],
        out_shape=[
            jax.ShapeDtypeStruct((bsz, seqlen, 3 * w), BF16),
            jax.ShapeDtypeStruct((bsz, seqlen, 3 * w), BF16),
            jax.ShapeDtypeStruct((bsz, seqlen, 3 * w), BF16),
            jax.ShapeDtypeStruct((bsz, seqlen, w), F32),
            jax.ShapeDtypeStruct((bsz, seqlen, w), F32),
            jax.ShapeDtypeStruct((bsz, seqlen, 2 * w), F32),
        ],
        compiler_params=_params(*semantics),
        name="rw_feat",
    )(*data, *wts)


def _stack_heads(x, lane_lo):
    return jnp.concatenate([jnp.where(lane_lo, x, 0.0), jnp.where(lane_lo, 0.0, x)], axis=0)


def _rw_scan_kernel(*refs, backward, final, steps_per_column):
    scatter = steps_per_column > 0
    if scatter:
        sh_ref, kb_ref, cm_ref, s0_ref, fin_ref, yb_ref, ln_ref, y_hbm, sn_ref, st_ref, y_buf, y_sems = refs
    elif final:
        sh_ref, kb_ref, cm_ref, s0_ref, fin_ref, yb_ref, ln_ref, y_ref, sn_ref, st_ref = refs
    else:
        sh_ref, kb_ref, cm_ref, s0_ref, y_ref, sn_ref, st_ref = refs
    v_ref = sh_ref
    j = pl.program_id(0)
    n_steps = pl.num_programs(0)

    @pl.when(j == 0)
    def _():
        st_ref[...] = s0_ref[...]

    if scatter:
        slot = j % 2
        step_rows = v_ref.shape[1]

        def out_copy(step, buf_slot):
            r0 = pl.multiple_of((step % steps_per_column) * step_rows, step_rows)
            dst = y_hbm.at[:, pl.ds(r0, step_rows), step // steps_per_column, :]
            return pltpu.make_async_copy(y_buf.at[buf_slot], dst, y_sems.at[buf_slot])

        @pl.when(j >= 2)
        def _():
            out_copy(j - 2, slot).wait()

        y_ref = y_buf.at[slot]

    bsz, c = v_ref.shape[0], RW_CHUNK
    n_sub = v_ref.shape[1] // c
    w = RW_WIDTH
    c2 = 2 * c
    ri2 = lax.broadcasted_iota(jnp.int32, (c2, c2), 0)
    ci2 = lax.broadcasted_iota(jnp.int32, (c2, c2), 1)
    incl = (ri2 <= ci2) if backward else (ri2 >= ci2)
    strict = (ri2 < ci2) if backward else (ri2 > ci2)
    pi =lax.broadcasted_iota(jnp.int32, (RW_PAIR, RW_PAIR), 0)
    pj = lax.broadcasted_iota(jnp.int32, (RW_PAIR, RW_PAIR), 1)
    lane_lo = lax.broadcasted_iota(jnp.int32, (c, RW_PAIR), 1) < RW_HEAD_DIM

    def head_mean(t):
        lo = jnp.sum(jnp.where(lane_lo, t, 0.0), axis=-1, keepdims=True)
        hi = jnp.sum(jnp.where(lane_lo, 0.0, t), axis=-1, keepdims=True)
        return jnp.where(lane_lo, lo, hi) * (1.0 / RW_HEAD_DIM)

    def pair_chunk(rts, kts, kis, bis, kds, bds, vs, w_last, state, turn):
        g = _dot_nt(jnp.concatenate([kts, rts], axis=0), jnp.concatenate([bis, kis], axis=0))
        yield
        a_b = jnp.where(strict, g[:c2, :c2], 0.0)
        a_k = jnp.where(strict, g[:c2, c2:], 0.0)
        a_rb = jnp.where(incl, g[c2:, :c2], 0.0)
        a_rk = jnp.where(incl, g[c2:, c2:], 0.0)
        x = _bdot(a_k, vs)
        t_inv = yield from _unit_tri_inverse(a_b, c)
        uk = _bdot(t_inv, jnp.concatenate([x.astype(BF16), kts], axis=1))
        yield
        zeros = jnp.zeros((c2, RW_PAIR), BF16)
        rhs = jnp.concatenate([jnp.concatenate([vs, zeros], axis=1), uk.astype(BF16)], axis=0)
        yr = _bdot(jnp.concatenate([a_rk, -a_rb], axis=1), rhs)
        nm = _dot_tn(jnp.concatenate([kds, -bds], axis=0), rhs)
        yield
        m_full = nm[:, RW_PAIR:] + jnp.where(pi == pj, w_last, 0.0)
        r_hat = rts.astype(F32) + yr[:, RW_PAIR:]
        while state["turn"] != turn:
            yield
        yh = _bdot(jnp.concatenate([r_hat, m_full], axis=0), state["h"])
        ys = yh[:c2] + yr[:, :RW_PAIR]
        state["h"] = yh[c2:] + nm[:, :RW_PAIR]
        state["turn"] = turn + 1
        return ys[:c] + ys[c:]

    pairs = [(b, p) for b in range(bsz) for p in range(RW_PAIRS)]
    states = {key: {"h": st_ref[key[0], key[1]], "turn": 0} for key in pairs}
    work, gens = [], []
    last = 0 if backward else c - 1
    for turn in range(n_sub):
        sub = n_sub - 1 - turn if backward else turn
        rows = slice(sub * c, (sub + 1) * c)
        for b in range(bsz):
            cum = cm_ref[b, rows, :]
            cum_last = cum[last:last + 1, :]
            e_in, e_out, e_tail = jnp.exp(cum), jnp.exp(-cum), jnp.exp(cum_last - cum)
            k_dir, b_dir = kb_ref[b, rows, 0:w], kb_ref[b, rows, w:2 * w]
            scaled = [sh_ref[b, rows, 0:w] * e_in,
                      sh_ref[b, rows, w:2 * w] * jnp.exp(cum - kb_ref[b, rows, 2 * w:3 * w]),
                      k_dir * e_out, b_dir * e_out, k_dir * e_tail, b_dir * e_tail]
            w_last = jnp.exp(cum_last)
            for p in range(RW_PAIRS):
                sl = slice(p * RW_PAIR, (p + 1) * RW_PAIR)
                work.append((b, rows, sl))
                operands = [_stack_heads(t[:, sl].astype(BF16), lane_lo) for t in scaled]
                vs = _stack_heads(sh_ref[b, rows, 2 * w + sl.start:2 * w + sl.stop], lane_lo)
                gens.append(pair_chunk(*operands, vs, w_last[:, sl], states[(b, p)], turn))
    results = _run_lockstep(gens)
    for (b, rows, sl), y in zip(work, results):
        if final:
            y = y + yb_ref[b, rows, sl] + fin_ref[b, rows, w + sl.start:w + sl.stop]
            yc = y - head_mean(y)
            y = yc * lax.rsqrt(head_mean(yc * yc) + RW_GN_EPS) * ln_ref[0:1, sl] + ln_ref[1:2, sl]
            y = y * fin_ref[b, rows, sl]
        y_ref[b, rows, sl] = y
    for b, p in pairs:
        st_ref[b, p] = states[(b, p)]["h"]
    if scatter:
        out_copy(j, slot).start()

    @pl.when(j == n_steps - 1)
    def _():
        sn_ref[...] = st_ref[...]
        if scatter:
            out_copy(j, slot).wait()

            @pl.when(j >= 1)
            def _():
                out_copy(j - 1, 1 - slot).wait()


def _rw_scan(shared, per_dir, cum, state0, n_cols, *, backward, fin=None, y_other=None, ln=None):
    bsz, seqlen, _ = shared.shape
    rows_per_col = seqlen // n_cols
    n_sub = _chunks_per_step(rows_per_col // RW_CHUNK)
    c = RW_CHUNK * n_sub
    n = seqlen // c
    w = RW_WIDTH
    final = fin is not None
    blk = (lambda j: n - 1 - j) if backward else (lambda j: j)
    idx = lambda j: (0, blk(j), 0)
    st_spec = pl.BlockSpec(state0.shape, lambda j: (0, 0, 0, 0))
    in_specs = [pl.BlockSpec((bsz, c, 3 * w), idx), pl.BlockSpec((bsz, c, 3 * w), idx),
                pl.BlockSpec((bsz, c, w), idx), st_spec]
    args = [shared, per_dir, cum, state0]
    scratch = [pltpu.VMEM(state0.shape, F32)]
    out_shape_y = jax.ShapeDtypeStruct((bsz, seqlen, w), F32)
    y_spec = pl.BlockSpec((bsz, c, w), idx)
    steps_per_column = 0
    if final:
        assert not backward
        in_specs += [
            pl.BlockSpec((bsz, c, 2 * w), idx),
            pl.BlockSpec((bsz, c, w), idx),
            pl.BlockSpec((2, w), lambda j: (0, 0)),
        ]
        args += [fin, y_other, ln]
        if n_cols > 1:
            steps_per_column = rows_per_col // c
            out_shape_y = jax.ShapeDtypeStruct((bsz, rows_per_col, n_cols, w), F32)
            y_spec = pl.BlockSpec(memory_space=pl.ANY)
            scratch += [pltpu.VMEM((2, bsz, c, w), F32), pltpu.SemaphoreType.DMA((2,))]
    y, state = pl.pallas_call(
        functools.partial(_rw_scan_kernel, backward=backward, final=final, steps_per_column=steps_per_column),
        grid=(n,),
        in_specs=in_specs,
        out_specs=[y_spec, st_spec],
        out_shape=[out_shape_y, jax.ShapeDtypeStruct(state0.shape, F32)],
        scratch_shapes=scratch,
        compiler_params=_params("arbitrary"),
        name="rw_scan_bwd" if backward else "rw_scan_fwd",
    )(*args)
    return y.reshape(bsz, seqlen, w), state


def _pad_rows(x, rows):
    return jnp.pad(x, ((0, rows - x.shape[0]), (0, 0)))


def _block_diag2(m):
    z = jnp.zeros_like(m[0])
    return jnp.concatenate([jnp.concatenate([m[0], z], axis=1), jnp.concatenate([z, m[1]], axis=1)], axis=0)


def _token_mixing(h_ctx, h_lat, mod_ctx, mod_lat, gain, lw, layer, *, tm_lat, tm_ctx):
    bsz = h_lat.shape[0]
    n_cols = GRID_W
    outs = []
    dn_state = jnp.zeros((2, bsz, DN_HEADS, DN_HEAD_DIM, DN_HEAD_DIM), F32)
    rw_state = jnp.zeros((2, bsz, RW_PAIRS, RW_PAIR, RW_PAIR), F32)
    dn_states = [dn_state[0], dn_state[1]]
    rw_states = [rw_state[0], rw_state[1]]
    for h, mod, tm, cols in ((h_ctx, mod_ctx, tm_ctx, 1), (h_lat, mod_lat, tm_lat, n_cols)):
        p_dn, p_rw = _proj(h, mod, gain, lw["w_dn"], lw["w_rw"], layer, tm=min(tm, 512))
        qkv, gb = _dn_features(p_dn, lw["conv_w"], lw["alog_row"], lw["dt_row"], rows=min(512, h.shape[1]))
        o_b, dn_states[1] = _dn_scan(qkv, gb, dn_states[1], backward=True)
        dn_out, dn_states[0] = _dn_scan(qkv, gb, dn_states[0], backward=False, p_dn=p_dn, o_other=o_b,
                                        norm_w=lw["dn_norm"])
        shared, dir0, dir1, cum0, cum1, fin = _rw_features(p_rw, cols, lw["rw_feat"])
        y_b, rw_states[1] = _rw_scan(shared, dir1, cum1, rw_states[1], cols, backward=True)
        rw_out, rw_states[0] = _rw_scan(shared, dir0, cum0, rw_states[0], cols, backward=False, fin=fin,
                                        y_other=y_b, ln=lw["rw_ln"])
        outs.append((dn_out, rw_out))
    return outs


def kernel(x, c, ctx, c_ctx, ada_w, ada_b, norm_g, ffn_w13, ffn_w2, w_in, dn_conv, dn_a_log, dn_dt_bias, dn_norm,
           rw_mu, rw_w0, rw_w2, rw_a0, rw_a2, rw_g2, rw_kk, rw_ka, rw_u, rw_ln, w_out, final_norm):
    depth = ada_w.shape[0]
    bsz, seqlen, d = x.shape
    ctx_len = ctx.shape[1]
    tm_lat, tm_ctx = 1024, ctx_len
    tm_mix = 512

    s_rows = jnp.concatenate([c_ctx[None, :], c, jnp.zeros((8 - 1 - bsz, d), F32)], axis=0)
    mods = _adaln(s_rows, ada_w, ada_b).reshape(depth, 8, N_MOD, d)

    w13_bf, w2_bf = ffn_w13.astype(BF16), ffn_w2.astype(BF16)
    dn_cols = 4 * DN_WIDTH + 4 * DN_HEADS
    w_dn = jnp.pad(w_in[:, :, :dn_cols], ((0, 0), (0, 0), (0, DN_P_WIDTH - dn_cols))).astype(BF16)
    w_rw = w_in[:, :, dn_cols:].astype(BF16)
    w_out_bf = w_out.astype(BF16)
    pad_lanes = lambda v: jnp.pad(v.reshape(1, -1), ((0, 0), (0, DN_AB_PAD - v.size)))

    h_lat, h_ctx = x, ctx
    for i in range(depth):
        last = i == depth - 1
        mod_ctx = jnp.broadcast_to(mods[i, 0:1], (bsz, N_MOD, d))
        mod_lat = mods[i, 1:1 + bsz]
        lw = {
            "w_dn": w_dn, "w_rw": w_rw,
            "conv_w": _pad_rows(dn_conv[i], 8),
            "alog_row": pad_lanes(dn_a_log[i]), "dt_row": pad_lanes(dn_dt_bias[i]),
            "dn_norm": dn_norm[i],
            "rw_feat": (
                _pad_rows(rw_mu[i], 8),
                rw_w0[i].reshape(1, -1), _block_diag2(rw_w2[i]).astype(BF16),
                rw_a0[i].reshape(1, -1), _block_diag2(rw_a2[i]).astype(BF16),
                rw_g2[i].astype(BF16),
                rw_kk[i].reshape(1, -1), rw_ka[i].reshape(1, -1), rw_u[i].reshape(1, -1),
            ),
            "rw_ln": rw_ln[i],
        }
        h_lat = _ffn(h_lat, mod_lat, norm_g[i, 0], w13_bf, w2_bf, i, 0, row0=0, tm=tm_lat)
        h_ctx = _ffn(h_ctx, mod_ctx, norm_g[i, 0], w13_bf, w2_bf, i, 0, row0=0, tm=tm_ctx)
        mix_ctx, mix_lat = _token_mixing(h_ctx, h_lat, mod_ctx, mod_lat, norm_g[i, 1], lw, i,
                                         tm_lat=tm_lat, tm_ctx=tm_ctx)
        h_lat = _ffn(h_lat, mod_lat, norm_g[i, 2], w13_bf, w2_bf, i, 1, row0=6, tm=tm_mix,
                     mix=(*mix_lat, w_out_bf), final_gain=final_norm if last else None)
        if not last:
            h_ctx = _ffn(h_ctx, mod_ctx, norm_g[i, 2], w13_bf, w2_bf, i, 1, row0=6, tm=tm_ctx,
                         mix=(*mix_ctx, w_out_bf))
    return h_lat
```

```python
import functools

import jax
import jax.numpy as jnp
from jax import lax
from jax.experimental import pallas as pl
from jax.experimental.pallas import tpu as pltpu

F32 = jnp.float32
BF16 = jnp.bfloat16
HIGHEST = lax.Precision.HIGHEST

NORM_EPS = 1e-6
RW_GN_EPS = 64e-5
N_MOD = 9
GRID_W = 64
SHORT_CONV = 5

DN_HEADS = 4
DN_HEAD_DIM = 128
DN_WIDTH = DN_HEADS * DN_HEAD_DIM
DN_AB_PAD = 128
DN_P_WIDTH = 4 * DN_WIDTH + DN_AB_PAD
DN_CHUNK = 128

RW_HEADS = 8
RW_HEAD_DIM = 64
RW_WIDTH = RW_HEADS * RW_HEAD_DIM
RW_LORA = 128
RW_P_WIDTH = 3 * RW_WIDTH + 3 * RW_LORA
RW_CHUNK = 64
RW_PAIR = 2 * RW_HEAD_DIM
RW_PAIRS = RW_WIDTH // RW_PAIR

SCAN_CHUNKS_PER_STEP = 2
HALO = 8
V7X_VMEM_LIMIT = 56 * 1024 * 1024


def _params(*sem):
    return pltpu.CompilerParams(dimension_semantics=sem, vmem_limit_bytes=V7X_VMEM_LIMIT)


def _dot(a, b, precision=None):
    return jnp.dot(a, b, preferred_element_type=F32, precision=precision)


def _dot_nt(a, b):
    return lax.dot_general(a, b, (((1,), (1,)), ((), ())), preferred_element_type=F32)


def _dot_tn(a, b):
    return lax.dot_general(a, b, (((0,), (0,)), ((), ())), preferred_element_type=F32)


def _bdot(a, b):
    return _dot(a.astype(BF16), b.astype(BF16))


def _dot_split3(a01, x):
    x1, x2, x3 = _split3(x)
    return _dot(a01, x1) + _dot(a01, x2) + _dot(a01, x3)


def _split3(x):
    x1 = x.astype(BF16)
    r1 = x - x1.astype(F32)
    x2 = r1.astype(BF16)
    x3 = (r1 - x2.astype(F32)).astype(BF16)
    return x1, x2, x3


def _silu(x):
    return x * jax.nn.sigmoid(x)


def _softplus(x):
    return jnp.maximum(x, 0.0) + jnp.log1p(jnp.exp(-jnp.abs(x)))


def _rms_modulate(h, gain, shift, scale):
    y = h * lax.rsqrt(jnp.mean(h * h, axis=-1, keepdims=True) + NORM_EPS) * gain
    return y * (1.0 + scale) + shift


def _unit_tri_inverse(a, size):
    n = a.shape[0]
    diff = lax.broadcasted_iota(jnp.int32, (n, n), 0) ^ lax.broadcasted_iota(jnp.int32, (n, n), 1)
    t = jnp.where(diff == 0, 1.0, jnp.where(diff == 1, -a, 0.0))
    level = 1
    while (1 << level) < size:
        l = jnp.where((diff >> level) == 1, a, 0.0)
        tl = _bdot(t, l)
        yield
        t = t - _bdot(tl, t)
        yield
        level += 1
    return t


def _chunks_per_step(n_chunks):
    return SCAN_CHUNKS_PER_STEP if n_chunks % SCAN_CHUNKS_PER_STEP == 0 else 1


def _run_lockstep(gens):
    results = [None] * len(gens)
    live = list(enumerate(gens))
    while live:
        still = []
        for i, g in live:
            try:
                next(g)
                still.append((i, g))
            except StopIteration as done:
                results[i] = done.value
        live = still
    return results


def _shift_rows(x, halo_prev, halo_next, s):
    rows = x.shape[0]
    row8 = lax.broadcasted_iota(jnp.int32, (HALO, x.shape[1]), 0)
    back = pltpu.roll(x, s, 0)
    hp = pltpu.roll(halo_prev, s, 0)
    back = jnp.concatenate([jnp.where(row8 < s, hp, back[:HALO]), back[HALO:]], axis=0)
    fwd = pltpu.roll(x, rows - s, 0)
    hn = pltpu.roll(halo_next, HALO - s, 0)
    fwd = jnp.concatenate([fwd[:rows - HALO], jnp.where(row8 >= HALO - s, hn, fwd[rows - HALO:])], axis=0)
    return back, fwd


def _adaln_kernel(s_ref, w_ref, b_ref, o_ref):
    s = _silu(s_ref[...])
    o_ref[0] = _dot(s, w_ref[0], HIGHEST) + b_ref[0]


def _adaln(s_rows, ada_w, ada_b):
    depth, d, n = ada_w.shape
    tn = n // 8
    return pl.pallas_call(
        _adaln_kernel,
        grid=(depth, n // tn),
        in_specs=[
            pl.BlockSpec((8, d), lambda l, j: (0, 0)),
            pl.BlockSpec((1, d, tn), lambda l, j: (l, 0, j)),
            pl.BlockSpec((1, 1, tn), lambda l, j: (l, 0, j)),
        ],
        out_specs=pl.BlockSpec((1, 8, tn), lambda l, j: (l, 0, j)),
        out_shape=jax.ShapeDtypeStruct((depth, 8, n), F32),
        compiler_params=_params("parallel", "parallel"),
        name="adaln",
    )(s_rows, ada_w, ada_b.reshape(depth, 1, n))


def _ffn_resident_kernel(h_ref, mod_ref, g_ref, w1_ref, w3_ref, w2_ref, o_ref, *, row0):
    h = h_ref[0]
    xn = _rms_modulate(h, g_ref[...], mod_ref[0, row0:row0 + 1, :], mod_ref[0, row0 + 1:row0 + 2, :]).astype(BF16)
    act = (_silu(_dot(xn, w1_ref[...])) * _dot(xn, w3_ref[...])).astype(BF16)
    o_ref[0] = h + 0.5 * mod_ref[0, row0 + 2:row0 + 3, :] * _dot(act, w2_ref[...])


def _ffn_resident(h, mod, gain, w13, w2, layer, which, *, row0, tm):
    bsz, seqlen, d = h.shape
    dff = w2.shape[2]
    once = pl.Buffered(1)
    tok = pl.BlockSpec((1, tm, d), lambda b, i: (b, i, 0))
    return pl.pallas_call(
        functools.partial(_ffn_resident_kernel, row0=row0),
        grid=(bsz, seqlen // tm),
        in_specs=[
            tok,
            pl.BlockSpec((1, N_MOD, d), lambda b, i: (b, 0, 0)),
            pl.BlockSpec((1, d), lambda b, i: (0, 0)),
            pl.BlockSpec((None, None, d, dff), lambda b, i: (layer, which, 0, 0), pipeline_mode=once),
            pl.BlockSpec((None, None, d, dff), lambda b, i: (layer, which, 0, 1), pipeline_mode=once),
            pl.BlockSpec((None, None, dff, d), lambda b, i: (layer, which, 0, 0), pipeline_mode=once),
        ],
        out_specs=tok,
        out_shape=jax.ShapeDtypeStruct(h.shape, F32),
        compiler_params=_params("parallel", "parallel"),
        name="ffn_resident",
    )(h, mod, gain.reshape(1, d), w13, w13, w2)


def _ffn_kernel(h_ref, mod_ref, g_ref, w1_ref, w3_ref, w2_ref, *rest, row0, mixed, final_norm):
    rest = list(rest)
    if mixed:
        dn_ref, rw_ref, wdn_ref, wrw_ref = rest[:4]
        rest = rest[4:]
    if final_norm:
        fg_ref = rest.pop(0)
    o_ref, xn_ref, acc_ref = rest[:3]
    res_ref = rest[3] if mixed else None
    k = pl.program_id(2)

    @pl.when(k == 0)
    def _():
        h = h_ref[0]
        if mixed:
            mix = _dot(dn_ref[0].astype(BF16), wdn_ref[...]) + _dot(rw_ref[0].astype(BF16), wrw_ref[...])
            h = h + mod_ref[0, 5:6, :] * mix
            res_ref[...] = h
        xn = _rms_modulate(h, g_ref[...], mod_ref[0, row0:row0 + 1, :], mod_ref[0, row0 + 1:row0 + 2, :])
        xn_ref[...] = xn.astype(BF16)
        acc_ref[...] = jnp.zeros_like(acc_ref)

    xn = xn_ref[...]
    gate = _dot(xn, w1_ref[...])
    up = _dot(xn, w3_ref[...])
    acc_ref[...] += _dot((_silu(gate) * up).astype(BF16), w2_ref[...])

    @pl.when(k == pl.num_programs(2) - 1)
    def _():
        res = res_ref[...] if mixed else h_ref[0]
        out = res + 0.5 * mod_ref[0, row0 + 2:row0 + 3, :] * acc_ref[...]
        if final_norm:
            out = out * lax.rsqrt(jnp.mean(out * out, axis=-1, keepdims=True) + NORM_EPS) * fg_ref[...]
        o_ref[0] = out


def _ffn(h, mod, gain, w13, w2, layer, which, *, row0, tm, mix=None, final_gain=None):
    bsz, seqlen, d = h.shape
    dff = w2.shape[2]
    tf = dff // 2
    nk = dff // tf
    row = pl.BlockSpec((1, d), lambda b, i, k: (0, 0))
    tok = lambda width: pl.BlockSpec((1, tm, width), lambda b, i, k: (b, i, 0))
    in_specs = [
        tok(d),
        pl.BlockSpec((1, N_MOD, d), lambda b, i, k: (b, 0, 0)),
        row,
        pl.BlockSpec((None, None, d, tf), lambda b, i, k: (layer, which, 0, k)),
        pl.BlockSpec((None, None, d, tf), lambda b, i, k: (layer, which, 0, k + nk)),
        pl.BlockSpec((None, None, tf, d), lambda b, i, k: (layer, which, k, 0)),
    ]
    args = [h, mod, gain.reshape(1, d), w13, w13, w2]
    scratch = [pltpu.VMEM((tm, d), BF16), pltpu.VMEM((tm, d), F32)]
    if mix is not None:
        dn, rw, w_out = mix
        in_specs += [
            tok(DN_WIDTH), tok(RW_WIDTH),
            pl.BlockSpec((None, DN_WIDTH, d), lambda b, i, k: (layer, 0, 0)),
            pl.BlockSpec((None, RW_WIDTH, d), lambda b, i, k: (layer, 1, 0)),
        ]
        args += [dn, rw, w_out, w_out]
        scratch.append(pltpu.VMEM((tm, d), F32))
    if final_gain is not None:
        in_specs.append(row)
        args.append(final_gain.reshape(1, d))
    return pl.pallas_call(
        functools.partial(_ffn_kernel, row0=row0, mixed=mix is not None, final_norm=final_gain is not None),
        grid=(bsz, seqlen // tm, nk),
        in_specs=in_specs,
        out_specs=tok(d),
        out_shape=jax.ShapeDtypeStruct(h.shape, F32),
        scratch_shapes=scratch,
        compiler_params=_params("parallel", "parallel", "arbitrary"),
        name="ffn_mix" if mix is not None else "ffn",
    )(*args)


def _proj_kernel(h_ref, mod_ref, g_ref, wdn_ref, wrw_ref, pdn_ref, prw_ref):
    xn = _rms_modulate(h_ref[0], g_ref[...], mod_ref[0, 3:4, :], mod_ref[0, 4:5, :]).astype(BF16)
    pdn_ref[0] = _dot(xn, wdn_ref[...])
    prw_ref[0] = _dot(xn, wrw_ref[...])


def _proj(h, mod, gain, w_dn, w_rw, layer, *, tm):
    bsz, seqlen, d = h.shape
    return pl.pallas_call(
        _proj_kernel,
        grid=(bsz, seqlen // tm),
        in_specs=[
            pl.BlockSpec((1, tm, d), lambda b, i: (b, i, 0)),
            pl.BlockSpec((1, N_MOD, d), lambda b, i: (b, 0, 0)),
            pl.BlockSpec((1, d), lambda b, i: (0, 0)),
            pl.BlockSpec((None, d, DN_P_WIDTH), lambda b, i: (layer, 0, 0)),
            pl.BlockSpec((None, d, RW_P_WIDTH), lambda b, i: (layer, 0, 0)),
        ],
        out_specs=[
            pl.BlockSpec((1, tm, DN_P_WIDTH), lambda b, i: (b, i, 0)),
            pl.BlockSpec((1, tm, RW_P_WIDTH), lambda b, i: (b, i, 0)),
        ],
        out_shape=[
            jax.ShapeDtypeStruct((bsz, seqlen, DN_P_WIDTH), F32),
            jax.ShapeDtypeStruct((bsz, seqlen, RW_P_WIDTH), F32),
        ],
        compiler_params=_params("parallel", "parallel"),
        name="proj",
    )(h, mod, gain.reshape(1, d), w_dn, w_rw)


def _dn_feat_kernel(x_ref, xp_ref, xn_ref, ab_ref, cw_ref, alog_ref, dt_ref, qkv_ref, gb_ref):
    i = pl.program_id(1)
    x = x_ref[0]
    hp = jnp.where(i == 0, 0.0, xp_ref[0])
    hn = jnp.where(i == pl.num_programs(1) - 1, 0.0, xn_ref[0])
    b1, f1 = _shift_rows(x, hp, hn, 1)
    b2, f2 = _shift_rows(x, hp, hn, 2)
    conv = cw_ref[0:1] * b2 + cw_ref[1:2] * b1 + cw_ref[2:3] * x + cw_ref[3:4] * f1 + cw_ref[4:5] * f2
    act = _silu(conv)
    for hd in range(2 * DN_HEADS):
        t = act[:, hd * DN_HEAD_DIM:(hd + 1) * DN_HEAD_DIM]
        t = t * lax.rsqrt(jnp.sum(t * t, axis=-1, keepdims=True) + NORM_EPS)
        if hd < DN_HEADS:
            t = t * DN_HEAD_DIM ** -0.5
        qkv_ref[0, :, hd * DN_HEAD_DIM:(hd + 1) * DN_HEAD_DIM] = t.astype(BF16)
    qkv_ref[0, :, 2 * DN_WIDTH:] = act[:, 2 * DN_WIDTH:].astype(BF16)
    ab = ab_ref[0]
    lane = lax.broadcasted_iota(jnp.int32, ab.shape, 1)
    g = -jnp.exp(alog_ref[...]) * _softplus(ab + dt_ref[...])
    rows = x.shape[0]
    ri = lax.broadcasted_iota(jnp.int32, (rows, rows), 0)
    ci = lax.broadcasted_iota(jnp.int32, (rows, rows), 1)
    same = (ri // DN_CHUNK) == (ci // DN_CHUNK)
    gc_f = _dot_split3(jnp.where(same & (ri >= ci), 1.0, 0.0).astype(BF16), g)
    gc_b = _dot_split3(jnp.where(same & (ri <= ci), 1.0, 0.0).astype(BF16), g)
    gb_ref[0] = jnp.where(lane < DN_HEADS, gc_f, jnp.where(lane < 2 * DN_HEADS, gc_b, jax.nn.sigmoid(ab)))


def _dn_features(p_dn, conv_w, alog_row, dt_row, *, rows):
    bsz, seqlen, _ = p_dn.shape
    nblk = seqlen // rows
    hpb = rows // HALO
    return pl.pallas_call(
        _dn_feat_kernel,
        grid=(bsz, nblk),
        in_specs=[
            pl.BlockSpec((1, rows, 3 * DN_WIDTH), lambda b, i: (b, i, 0)),
            pl.BlockSpec((1, HALO, 3 * DN_WIDTH), lambda b, i: (b, jnp.maximum(i * hpb - 1, 0), 0)),
            pl.BlockSpec((1, HALO, 3 * DN_WIDTH), lambda b, i: (b, jnp.minimum((i + 1) * hpb, nblk * hpb - 1), 0)),
            pl.BlockSpec((1, rows, DN_AB_PAD), lambda b, i: (b, i, 4 * DN_WIDTH // DN_AB_PAD)),
            pl.BlockSpec((8, 3 * DN_WIDTH), lambda b, i: (0, 0)),
            pl.BlockSpec((1, DN_AB_PAD), lambda b, i: (0, 0)),
            pl.BlockSpec((1, DN_AB_PAD), lambda b, i: (0, 0)),
        ],
        out_specs=[
            pl.BlockSpec((1, rows, 3 * DN_WIDTH), lambda b, i: (b, i, 0)),
            pl.BlockSpec((1, rows, DN_AB_PAD), lambda b, i: (b, i, 0)),
        ],
        out_shape=[
            jax.ShapeDtypeStruct((bsz, seqlen, 3 * DN_WIDTH), BF16),
            jax.ShapeDtypeStruct((bsz, seqlen, DN_AB_PAD), F32),
        ],
        compiler_params=_params("parallel", "parallel"),
        name="dn_feat",
    )(p_dn, p_dn, p_dn, p_dn, conv_w, alog_row, dt_row)


def _dn_scan_kernel(*refs, backward, final):
    if final:
        qkv_ref, gb_ref, s0_ref, z_ref, ob_ref, nw_ref, o_ref, sn_ref, st_ref = refs
    else:
        qkv_ref, gb_ref, s0_ref, o_ref, sn_ref, st_ref = refs
    j = pl.program_id(0)

    @pl.when(j == 0)
    def _():
        st_ref[...] = s0_ref[...]

    bsz, c = qkv_ref.shape[0], DN_CHUNK
    n_sub = qkv_ref.shape[1] // c
    ri = lax.broadcasted_iota(jnp.int32, (c, c), 0)
    ci = lax.broadcasted_iota(jnp.int32, (c, c), 1)
    incl = (ri <= ci) if backward else (ri >= ci)
    strict = (ri < ci) if backward else (ri > ci)
    d = 1 if backward else 0
    last = 0 if backward else c - 1

    def chunk(q, k, v, beta, gcc, gcr, gcl, state, turn):
        decay = jnp.where(incl, jnp.exp(jnp.where(incl, gcc - gcr, 0.0)), 0.0)
        kb = k.astype(BF16)
        kq = _dot_nt(jnp.concatenate([kb, q.astype(BF16)], axis=0), kb)
        yield
        a = jnp.where(strict, kq[:c] * decay * beta, 0.0)
        attn = jnp.where(incl, kq[c:] * decay, 0.0)
        t_inv = yield from _unit_tri_inverse(a, c)
        e_gc = jnp.exp(gcc)
        wu = _bdot(t_inv, jnp.concatenate([k * (beta * e_gc), v * beta], axis=1))
        yield
        while state["turn"] != turn:
            yield
        s = state["s"]
        wq = _bdot(jnp.concatenate([wu[:, :DN_HEAD_DIM], q * e_gc], axis=0), s)
        yield
        v_new = wu[:, DN_HEAD_DIM:] - wq[:c]
        o = wq[c:] + _bdot(attn, v_new)
        k_dec = k * jnp.exp(gcl - gcc)
        state["s"] = s * jnp.exp(gcl) + _dot_tn(k_dec.astype(BF16), v_new.astype(BF16))
        state["turn"] = turn + 1
        return o

    heads = [(b, hd) for b in range(bsz) for hd in range(DN_HEADS)]
    head_cols = lambda base, hd: slice(base + hd * DN_HEAD_DIM, base + (hd + 1) * DN_HEAD_DIM)
    states = {key: {"s": st_ref[key[0], key[1]], "turn": 0} for key in heads}
    work, gens = [], []
    for turn in range(n_sub):
        r0 = (n_sub - 1 - turn if backward else turn) * c
        rows = slice(r0, r0 + c)
        gbs = [gb_ref[b, rows, :] for b in range(bsz)]
        gc_rows = [gb.T for gb in gbs]
        for b, hd in heads:
            col = DN_HEADS * d + hd
            work.append((b, hd, rows))
            gens.append(chunk(
                qkv_ref[b, rows, head_cols(0, hd)], qkv_ref[b, rows, head_cols(DN_WIDTH, hd)],
                qkv_ref[b, rows, head_cols(2 * DN_WIDTH, hd)],
                gbs[b][:, 2 * DN_HEADS + col:2 * DN_HEADS + col + 1],
                gbs[b][:, col:col + 1], gc_rows[b][col:col + 1, :], gbs[b][last:last + 1, col:col + 1],
                states[(b, hd)], turn))
    results = _run_lockstep(gens)
    for (b, hd, rows), o in zip(work, results):
        if final:
            o = o + ob_ref[b, rows, head_cols(0, hd)]
            o = o * lax.rsqrt(jnp.mean(o * o, axis=-1, keepdims=True) + NORM_EPS) * nw_ref[...]
            o = o * _silu(z_ref[b, rows, head_cols(0, hd)])
        o_ref[b, rows, head_cols(0, hd)] = o
    for b, hd in heads:
        st_ref[b, hd] = states[(b, hd)]["s"]

    @pl.when(j == pl.num_programs(0) - 1)
    def _():
        sn_ref[...] = st_ref[...]


def _dn_scan(qkv, gb, state0, *, backward, p_dn=None, o_other=None, norm_w=None):
    bsz, seqlen, _ = qkv.shape
    c = DN_CHUNK * _chunks_per_step(seqlen // DN_CHUNK)
    n = seqlen // c
    final = p_dn is not None
    idx = (lambda j: (0, n - 1 - j, 0)) if backward else (lambda j: (0, j, 0))
    st_spec = pl.BlockSpec((bsz, DN_HEADS, DN_HEAD_DIM, DN_HEAD_DIM), lambda j: (0, 0, 0, 0))
    in_specs = [
        pl.BlockSpec((bsz, c, 3 * DN_WIDTH), idx),
        pl.BlockSpec((bsz, c, DN_AB_PAD), idx),
        st_spec,
    ]
    args = [qkv, gb, state0]
    if final:
        zcol = 3 * DN_WIDTH // DN_WIDTH
        in_specs += [
            pl.BlockSpec((bsz, c, DN_WIDTH), lambda j: (0, idx(j)[1], zcol)),
            pl.BlockSpec((bsz, c, DN_WIDTH), idx),
            pl.BlockSpec((1, DN_HEAD_DIM), lambda j: (0, 0)),
        ]
        args += [p_dn, o_other, norm_w.reshape(1, DN_HEAD_DIM)]
    return pl.pallas_call(
        functools.partial(_dn_scan_kernel, backward=backward, final=final),
        grid=(n,),
        in_specs=in_specs,
        out_specs=[pl.BlockSpec((bsz, c, DN_WIDTH), idx), st_spec],
        out_shape=[
            jax.ShapeDtypeStruct((bsz, seqlen, DN_WIDTH), F32),
            jax.ShapeDtypeStruct(state0.shape, F32),
        ],
        scratch_shapes=[pltpu.VMEM(state0.shape, F32)],
        compiler_params=_params("arbitrary"),
        name="dn_scan_bwd" if backward else "dn_scan_fwd",
    )(*args)


def _head_sum(x, ones_bd):
    pieces = _split3(x)
    parts = []
    for p in range(RW_PAIRS):
        sl = slice(p * RW_PAIR, (p + 1) * RW_PAIR)
        parts.append(_dot(pieces[0][:, sl], ones_bd) + _dot(pieces[1][:, sl], ones_bd) + _dot(pieces[2][:, sl], ones_bd))
    return jnp.concatenate(parts, axis=1)


def _pair_ones():
    ri = lax.broadcasted_iota(jnp.int32, (RW_PAIR, RW_PAIR), 0)
    ci = lax.broadcasted_iota(jnp.int32, (RW_PAIR, RW_PAIR), 1)
    return ((ri // RW_HEAD_DIM) == (ci // RW_HEAD_DIM)).astype(F32)


def _rw_column_copies(p_hbm, xbuf, hbuf, sems, b, c, slot):
    rows, n_cols = p_hbm.shape[1], p_hbm.shape[2]
    return (
        pltpu.make_async_copy(p_hbm.at[b, :, c, :], xbuf.at[slot], sems.at[slot, 0]),
        pltpu.make_async_copy(p_hbm.at[b, pl.ds(rows - HALO, HALO), jnp.maximum(c - 1, 0), :], hbuf.at[slot, 0],
                              sems.at[slot, 1]),
        pltpu.make_async_copy(p_hbm.at[b, pl.ds(0, HALO), jnp.minimum(c + 1, n_cols - 1), :], hbuf.at[slot, 1],
                              sems.at[slot, 2]),
    )


def _rw_feat_kernel(*refs, gather_columns):
    c = pl.program_id(1)
    n_cols = pl.num_programs(1)
    if gather_columns:
        (p_hbm, mu_ref, w0_ref, w2_ref, a0_ref, a2_ref, g2_ref, kk_ref, ka_ref, u_ref,
         sh_ref, kb0_ref, kb1_ref, cm0_ref, cm1_ref, fin_ref, xbuf, hbuf, sems) = refs
        n = pl.program_id(0) * n_cols + c
        slot = n % 2

        @pl.when(n == 0)
        def _():
            for cp in _rw_column_copies(p_hbm, xbuf, hbuf, sems, pl.program_id(0), c, slot):
                cp.start()

        @pl.when(n + 1 < pl.num_programs(0) * n_cols)
        def _():
            for cp in _rw_column_copies(p_hbm, xbuf, hbuf, sems, (n + 1) // n_cols, (n + 1) % n_cols, 1 - slot):
                cp.start()

        for cp in _rw_column_copies(p_hbm, xbuf, hbuf, sems, pl.program_id(0), c, slot):
            cp.wait()
        x, halo_prev, halo_next = xbuf[slot], hbuf[slot, 0], hbuf[slot, 1]
    else:
        (x_ref, xp_ref, xn_ref, mu_ref, w0_ref, w2_ref, a0_ref, a2_ref, g2_ref, kk_ref, ka_ref, u_ref,
         sh_ref, kb0_ref, kb1_ref, cm0_ref, cm1_ref, fin_ref) = refs
        x, halo_prev, halo_next = x_ref[0], xp_ref[0], xn_ref[0]
    hp = jnp.where(c == 0, 0.0, halo_prev)
    hn = jnp.where(c == n_cols - 1, 0.0, halo_next)
    prev, nxt = _shift_rows(x, hp, hn, 1)
    p = x + mu_ref[0:1] * (prev - x) + mu_ref[1:2] * (nxt - x)
    w = RW_WIDTH
    r, k, v = p[:, :w], p[:, w:2 * w], p[:, 2 * w:3 * w]
    lw = p[:, 3 * w:3 * w + RW_LORA]
    la = p[:, 3 * w + RW_LORA:3 * w + 2 * RW_LORA]
    lg = p[:, 3 * w + 2 * RW_LORA:]
    w_log = -_softplus(-(w0_ref[...] + _bdot(jnp.tanh(lw), w2_ref[...]))) - 0.5
    log_decay = -jnp.exp(w_log)
    a = jax.nn.sigmoid(a0_ref[...] + _bdot(la, a2_ref[...]))
    gate = _bdot(jax.nn.sigmoid(lg), g2_ref[...])
    ones_bd = _pair_ones().astype(BF16)
    kx = k * kk_ref[...]
    kk = kx * lax.rsqrt(_head_sum(kx * kx, ones_bd) + NORM_EPS)
    sh_ref[0, :, 0 * w:1 * w] = r.astype(BF16)
    sh_ref[0, :, 1 * w:2 * w] = kk.astype(BF16)
    sh_ref[0, :, 2 * w:3 * w] = v.astype(BF16)
    rows = x.shape[0]
    ri = lax.broadcasted_iota(jnp.int32, (rows, rows), 0)
    ci = lax.broadcasted_iota(jnp.int32, (rows, rows), 1)
    same = (ri // RW_CHUNK) == (ci // RW_CHUNK)
    bonus = jnp.zeros_like(v)
    for d, (kb_ref, cm_ref) in enumerate(((kb0_ref, cm0_ref), (kb1_ref, cm1_ref))):
        a_d = a[:, d * w:(d + 1) * w]
        k_dir = k * (1.0 + (a_d - 1.0) * ka_ref[...])
        bonus = bonus + _head_sum(r * k_dir * u_ref[:, d * w:(d + 1) * w], ones_bd) * v
        ld = log_decay[:, d * w:(d + 1) * w]
        kb_ref[0, :, 0 * w:1 * w] = k_dir.astype(BF16)
        kb_ref[0, :, 1 * w:2 * w] = (a_d * kk).astype(BF16)
        kb_ref[0, :, 2 * w:3 * w] = ld.astype(BF16)
        tri = jnp.where(same & ((ri >= ci) if d == 0 else (ri <= ci)), 1.0, 0.0).astype(BF16)
        cm_ref[0] = _dot_split3(tri, ld)
    fin_ref[0, :, :w] = gate
    fin_ref[0, :, w:] = bonus


def _rw_features(p_rw, n_cols, wts):
    bsz, seqlen, _ = p_rw.shape
    rows = seqlen // n_cols
    hpb = rows // HALO
    const = lambda shape: pl.BlockSpec(shape, lambda b, c: (0, 0))
    w = RW_WIDTH
    out_spec = lambda width: pl.BlockSpec((1, rows, width), lambda b, c: (b, c, 0))
    gather_columns = n_cols > 1
    if gather_columns:
        data_specs = [pl.BlockSpec(memory_space=pl.ANY)]
        data = [p_rw.reshape(bsz, rows, n_cols, RW_P_WIDTH)]
        scratch = [pltpu.VMEM((2, rows, RW_P_WIDTH), F32), pltpu.VMEM((2, 2, HALO, RW_P_WIDTH), F32),
                   pltpu.SemaphoreType.DMA((2, 3))]
        semantics = ("arbitrary", "arbitrary")
    else:
        data_specs = [
            pl.BlockSpec((1, rows, RW_P_WIDTH), lambda b, c: (b, 0, 0)),
            pl.BlockSpec((1, HALO, RW_P_WIDTH), lambda b, c: (b, hpb - 1, 0)),
            pl.BlockSpec((1, HALO, RW_P_WIDTH), lambda b, c: (b, 0, 0)),
        ]
        data = [p_rw, p_rw, p_rw]
        scratch = []
        semantics = ("parallel", "parallel")
    return pl.pallas_call(
        functools.partial(_rw_feat_kernel, gather_columns=gather_columns),
        grid=(bsz, n_cols),
        in_specs=data_specs + [
            const((8, RW_P_WIDTH)),
            const((1, 2 * w)), const((RW_LORA, 2 * w)),
            const((1, 2 * w)), const((RW_LORA, 2 * w)),
            const((RW_LORA, w)),
            const((1, w)), const((1, w)), const((1, 2 * w)),
        ],
        scratch_shapes=scratch,
        out_specs=[out_spec(3 * w), out_spec(3 * w), out_spec(3 * w), out_spec(w), out_spec(w), out_spec(2 * w)],
        out_shape=[
            jax.ShapeDtypeStruct((bsz, seqlen, 3 * w), BF16),
            jax.ShapeDtypeStruct((bsz, seqlen, 3 * w), BF16),
            jax.ShapeDtypeStruct((bsz, seqlen, 3 * w), BF16),
            jax.ShapeDtypeStruct((bsz, seqlen, w), F32),
            jax.ShapeDtypeStruct((bsz, seqlen, w), F32),
            jax.ShapeDtypeStruct((bsz, seqlen, 2 * w), F32),
        ],
        compiler_params=_params(*semantics),
        name="rw_feat",
    )(*data, *wts)


def _stack_heads(x, lane_lo):
    return jnp.concatenate([jnp.where(lane_lo, x, 0.0), jnp.where(lane_lo, 0.0, x)], axis=0)


def _rw_scan_kernel(*refs, backward, final, steps_per_column):
    scatter = steps_per_column > 0
    if scatter:
        sh_ref, kb_ref, cm_ref, s0_ref, fin_ref, yb_ref, ln_ref, y_hbm, sn_ref, st_ref, y_buf, y_sems = refs
    elif final:
        sh_ref, kb_ref, cm_ref, s0_ref, fin_ref, yb_ref, ln_ref, y_ref, sn_ref, st_ref = refs
    else:
        sh_ref, kb_ref, cm_ref, s0_ref, y_ref, sn_ref, st_ref = refs
    v_ref = sh_ref
    j = pl.program_id(0)
    n_steps = pl.num_programs(0)

    @pl.when(j == 0)
    def _():
        st_ref[...] = s0_ref[...]

    if scatter:
        slot = j % 2
        step_rows = v_ref.shape[1]

        def out_copy(step, buf_slot):
            r0 = pl.multiple_of((step % steps_per_column) * step_rows, step_rows)
            dst = y_hbm.at[:, pl.ds(r0, step_rows), step // steps_per_column, :]
            return pltpu.make_async_copy(y_buf.at[buf_slot], dst, y_sems.at[buf_slot])

        @pl.when(j >= 2)
        def _():
            out_copy(j - 2, slot).wait()

        y_ref = y_buf.at[slot]

    bsz, c = v_ref.shape[0], RW_CHUNK
    n_sub = v_ref.shape[1] // c
    w = RW_WIDTH
    c2 = 2 * c
    ri2 = lax.broadcasted_iota(jnp.int32, (c2, c2), 0)
    ci2 = lax.broadcasted_iota(jnp.int32, (c2, c2), 1)
    incl = (ri2 <= ci2) if backward else (ri2 >= ci2)
    strict = (ri2 < ci2) if backward else (ri2 > ci2)
    pi =lax.broadcasted_iota(jnp.int32, (RW_PAIR, RW_PAIR), 0)
    pj = lax.broadcasted_iota(jnp.int32, (RW_PAIR, RW_PAIR), 1)
    lane_lo = lax.broadcasted_iota(jnp.int32, (c, RW_PAIR), 1) < RW_HEAD_DIM

    def head_mean(t):
        lo = jnp.sum(jnp.where(lane_lo, t, 0.0), axis=-1, keepdims=True)
        hi = jnp.sum(jnp.where(lane_lo, 0.0, t), axis=-1, keepdims=True)
        return jnp.where(lane_lo, lo, hi) * (1.0 / RW_HEAD_DIM)

    def pair_chunk(rts, kts, kis, bis, kds, bds, vs, w_last, state, turn):
        g = _dot_nt(jnp.concatenate([kts, rts], axis=0), jnp.concatenate([bis, kis], axis=0))
        yield
        a_b = jnp.where(strict, g[:c2, :c2], 0.0)
        a_k = jnp.where(strict, g[:c2, c2:], 0.0)
        a_rb = jnp.where(incl, g[c2:, :c2], 0.0)
        a_rk = jnp.where(incl, g[c2:, c2:], 0.0)
        x = _bdot(a_k, vs)
        t_inv = yield from _unit_tri_inverse(a_b, c)
        uk = _bdot(t_inv, jnp.concatenate([x.astype(BF16), kts], axis=1))
        yield
        zeros = jnp.zeros((c2, RW_PAIR), BF16)
        rhs = jnp.concatenate([jnp.concatenate([vs, zeros], axis=1), uk.astype(BF16)], axis=0)
        yr = _bdot(jnp.concatenate([a_rk, -a_rb], axis=1), rhs)
        nm = _dot_tn(jnp.concatenate([kds, -bds], axis=0), rhs)
        yield
        m_full = nm[:, RW_PAIR:] + jnp.where(pi == pj, w_last, 0.0)
        r_hat = rts.astype(F32) + yr[:, RW_PAIR:]
        while state["turn"] != turn:
            yield
        yh = _bdot(jnp.concatenate([r_hat, m_full], axis=0), state["h"])
        ys = yh[:c2] + yr[:, :RW_PAIR]
        state["h"] = yh[c2:] + nm[:, :RW_PAIR]
        state["turn"] = turn + 1
        return ys[:c] + ys[c:]

    pairs = [(b, p) for b in range(bsz) for p in range(RW_PAIRS)]
    states = {key: {"h": st_ref[key[0], key[1]], "turn": 0} for key in pairs}
    work, gens = [], []
    last = 0 if backward else c - 1
    for turn in range(n_sub):
        sub = n_sub - 1 - turn if backward else turn
        rows = slice(sub * c, (sub + 1) * c)
        for b in range(bsz):
            cum = cm_ref[b, rows, :]
            cum_last = cum[last:last + 1, :]
            e_in, e_out, e_tail = jnp.exp(cum), jnp.exp(-cum), jnp.exp(cum_last - cum)
            k_dir, b_dir = kb_ref[b, rows, 0:w], kb_ref[b, rows, w:2 * w]
            scaled = [sh_ref[b, rows, 0:w] * e_in,
                      sh_ref[b, rows, w:2 * w] * jnp.exp(cum - kb_ref[b, rows, 2 * w:3 * w]),
                      k_dir * e_out, b_dir * e_out, k_dir * e_tail, b_dir * e_tail]
            w_last = jnp.exp(cum_last)
            for p in range(RW_PAIRS):
                sl = slice(p * RW_PAIR, (p + 1) * RW_PAIR)
                work.append((b, rows, sl))
                operands = [_stack_heads(t[:, sl].astype(BF16), lane_lo) for t in scaled]
                vs = _stack_heads(sh_ref[b, rows, 2 * w + sl.start:2 * w + sl.stop], lane_lo)
                gens.append(pair_chunk(*operands, vs, w_last[:, sl], states[(b, p)], turn))
    results = _run_lockstep(gens)
    for (b, rows, sl), y in zip(work, results):
        if final:
            y = y + yb_ref[b, rows, sl] + fin_ref[b, rows, w + sl.start:w + sl.stop]
            yc = y - head_mean(y)
            y = yc * lax.rsqrt(head_mean(yc * yc) + RW_GN_EPS) * ln_ref[0:1, sl] + ln_ref[1:2, sl]
            y = y * fin_ref[b, rows, sl]
        y_ref[b, rows, sl] = y
    for b, p in pairs:
        st_ref[b, p] = states[(b, p)]["h"]
    if scatter:
        out_copy(j, slot).start()

    @pl.when(j == n_steps - 1)
    def _():
        sn_ref[...] = st_ref[...]
        if scatter:
            out_copy(j, slot).wait()

            @pl.when(j >= 1)
            def _():
                out_copy(j - 1, 1 - slot).wait()


def _rw_scan(shared, per_dir, cum, state0, n_cols, *, backward, fin=None, y_other=None, ln=None):
    bsz, seqlen, _ = shared.shape
    rows_per_col = seqlen // n_cols
    n_sub = _chunks_per_step(rows_per_col // RW_CHUNK)
    c = RW_CHUNK * n_sub
    n = seqlen // c
    w = RW_WIDTH
    final = fin is not None
    blk = (lambda j: n - 1 - j) if backward else (lambda j: j)
    idx = lambda j: (0, blk(j), 0)
    st_spec = pl.BlockSpec(state0.shape, lambda j: (0, 0, 0, 0))
    in_specs = [pl.BlockSpec((bsz, c, 3 * w), idx), pl.BlockSpec((bsz, c, 3 * w), idx),
                pl.BlockSpec((bsz, c, w), idx), st_spec]
    args = [shared, per_dir, cum, state0]
    scratch = [pltpu.VMEM(state0.shape, F32)]
    out_shape_y = jax.ShapeDtypeStruct((bsz, seqlen, w), F32)
    y_spec = pl.BlockSpec((bsz, c, w), idx)
    steps_per_column = 0
    if final:
        assert not backward
        in_specs += [
            pl.BlockSpec((bsz, c, 2 * w), idx),
            pl.BlockSpec((bsz, c, w), idx),
            pl.BlockSpec((2, w), lambda j: (0, 0)),
        ]
        args += [fin, y_other, ln]
        if n_cols > 1:
            steps_per_column = rows_per_col // c
            out_shape_y = jax.ShapeDtypeStruct((bsz, rows_per_col, n_cols, w), F32)
            y_spec = pl.BlockSpec(memory_space=pl.ANY)
            scratch += [pltpu.VMEM((2, bsz, c, w), F32), pltpu.SemaphoreType.DMA((2,))]
    y, state = pl.pallas_call(
        functools.partial(_rw_scan_kernel, backward=backward, final=final, steps_per_column=steps_per_column),
        grid=(n,),
        in_specs=in_specs,
        out_specs=[y_spec, st_spec],
        out_shape=[out_shape_y, jax.ShapeDtypeStruct(state0.shape, F32)],
        scratch_shapes=scratch,
        compiler_params=_params("arbitrary"),
        name="rw_scan_bwd" if backward else "rw_scan_fwd",
    )(*args)
    return y.reshape(bsz, seqlen, w), state


def _pad_rows(x, rows):
    return jnp.pad(x, ((0, rows - x.shape[0]), (0, 0)))


def _block_diag2(m):
    z = jnp.zeros_like(m[0])
    return jnp.concatenate([jnp.concatenate([m[0], z], axis=1), jnp.concatenate([z, m[1]], axis=1)], axis=0)


def _token_mixing(h_ctx, h_lat, mod_ctx, mod_lat, gain, lw, layer, *, tm_lat, tm_ctx):
    bsz = h_lat.shape[0]
    n_cols = GRID_W
    outs = []
    dn_state = jnp.zeros((2, bsz, DN_HEADS, DN_HEAD_DIM, DN_HEAD_DIM), F32)
    rw_state = jnp.zeros((2, bsz, RW_PAIRS, RW_PAIR, RW_PAIR), F32)
    dn_states = [dn_state[0], dn_state[1]]
    rw_states = [rw_state[0], rw_state[1]]
    for h, mod, tm, cols in ((h_ctx, mod_ctx, tm_ctx, 1), (h_lat, mod_lat, tm_lat, n_cols)):
        p_dn, p_rw = _proj(h, mod, gain, lw["w_dn"], lw["w_rw"], layer, tm=min(tm, 512))
        qkv, gb = _dn_features(p_dn, lw["conv_w"], lw["alog_row"], lw["dt_row"], rows=min(512, h.shape[1]))
        o_b, dn_states[1] = _dn_scan(qkv, gb, dn_states[1], backward=True)
        dn_out, dn_states[0] = _dn_scan(qkv, gb, dn_states[0], backward=False, p_dn=p_dn, o_other=o_b,
                                        norm_w=lw["dn_norm"])
        shared, dir0, dir1, cum0, cum1, fin = _rw_features(p_rw, cols, lw["rw_feat"])
        y_b, rw_states[1] = _rw_scan(shared, dir1, cum1, rw_states[1], cols, backward=True)
        rw_out, rw_states[0] = _rw_scan(shared, dir0, cum0, rw_states[0], cols, backward=False, fin=fin,
                                        y_other=y_b, ln=lw["rw_ln"])
        outs.append((dn_out, rw_out))
    return outs


def kernel(x, c, ctx, c_ctx, ada_w, ada_b, norm_g, ffn_w13, ffn_w2, w_in, dn_conv, dn_a_log, dn_dt_bias, dn_norm,
           rw_mu, rw_w0, rw_w2, rw_a0, rw_a2, rw_g2, rw_kk, rw_ka, rw_u, rw_ln, w_out, final_norm):
    depth = ada_w.shape[0]
    bsz, seqlen, d = x.shape
    ctx_len = ctx.shape[1]
    tm_lat, tm_ctx = 1024, ctx_len
    tm_mix = 512

    s_rows = jnp.concatenate([c_ctx[None, :], c, jnp.zeros((8 - 1 - bsz, d), F32)], axis=0)
    mods = _adaln(s_rows, ada_w, ada_b).reshape(depth, 8, N_MOD, d)

    w13_bf, w2_bf = ffn_w13.astype(BF16), ffn_w2.astype(BF16)
    dn_cols = 4 * DN_WIDTH + 4 * DN_HEADS
    w_dn = jnp.pad(w_in[:, :, :dn_cols], ((0, 0), (0, 0), (0, DN_P_WIDTH - dn_cols))).astype(BF16)
    w_rw = w_in[:, :, dn_cols:].astype(BF16)
    w_out_bf = w_out.astype(BF16)
    pad_lanes = lambda v: jnp.pad(v.reshape(1, -1), ((0, 0), (0, DN_AB_PAD - v.size)))

    h_lat, h_ctx = x, ctx
    for i in range(depth):
        last = i == depth - 1
        mod_ctx = jnp.broadcast_to(mods[i, 0:1], (bsz, N_MOD, d))
        mod_lat = mods[i, 1:1 + bsz]
        lw = {
            "w_dn": w_dn, "w_rw": w_rw,
            "conv_w": _pad_rows(dn_conv[i], 8),
            "alog_row": pad_lanes(dn_a_log[i]), "dt_row": pad_lanes(dn_dt_bias[i]),
            "dn_norm": dn_norm[i],
            "rw_feat": (
                _pad_rows(rw_mu[i], 8),
                rw_w0[i].reshape(1, -1), _block_diag2(rw_w2[i]).astype(BF16),
                rw_a0[i].reshape(1, -1), _block_diag2(rw_a2[i]).astype(BF16),
                rw_g2[i].astype(BF16),
                rw_kk[i].reshape(1, -1), rw_ka[i].reshape(1, -1), rw_u[i].reshape(1, -1),
            ),
            "rw_ln": rw_ln[i],
        }
        h_lat = _ffn_resident(h_lat, mod_lat, norm_g[i, 0], w13_bf, w2_bf, i, 0, row0=0, tm=512)
        h_ctx = _ffn(h_ctx, mod_ctx, norm_g[i, 0], w13_bf, w2_bf, i, 0, row0=0, tm=tm_ctx)
        mix_ctx, mix_lat = _token_mixing(h_ctx, h_lat, mod_ctx, mod_lat, norm_g[i, 1], lw, i,
                                         tm_lat=tm_lat, tm_ctx=tm_ctx)
        h_lat = _ffn(h_lat, mod_lat, norm_g[i, 2], w13_bf, w2_bf, i, 1, row0=6, tm=tm_mix,
                     mix=(*mix_lat, w_out_bf), final_gain=final_norm if last else None)
        if not last:
            h_ctx = _ffn(h_ctx, mod_ctx, norm_g[i, 2], w13_bf, w2_bf, i, 1, row0=6, tm=tm_ctx,
                         mix=(*mix_ctx, w_out_bf))
    return h_lat
```
